```python
import jax, jax.numpy as jnp
from jax import lax
import numpy as np

D_MODEL = 4096
BATCH = 8
SEQ = 2048
DEPTH = 2

D_BRANCH = 1024
N_BRANCH = 3
D_FF = 11008
CHUNK = 64
EPS = 1e-6

M_HEADS = 4
M_DV = D_BRANCH // M_HEADS
M_DK = M_DV // 2
M_CONV = 4
M_GATE_CAP = 15.0

R_HEAD_DIM = 64
R_HEADS = D_BRANCH // R_HEAD_DIM
R_W_RANK = 64
R_A_RANK = 64
R_G_RANK = 128
R_V_RANK = 32
R_GN_EPS = 64e-5

G_HEADS = 4
G_DV = D_BRANCH // G_HEADS
G_DK = G_DV // 2
G_RANK = 16
G_TAU = 16.0

M_SIZES = (M_HEADS * M_DK, M_HEADS * M_DK, D_BRANCH, D_BRANCH, M_HEADS, M_HEADS)
R_SIZES = (D_BRANCH, D_BRANCH, D_BRANCH, R_W_RANK, R_A_RANK, R_G_RANK)
G_SIZES = (G_HEADS * G_DK, G_HEADS * G_DK, D_BRANCH, G_RANK, D_BRANCH)
R_COLS = sum(R_SIZES)
IN_SIZES = M_SIZES + (R_COLS,) + G_SIZES + (N_BRANCH * D_MODEL,)
N_IN = sum(IN_SIZES)
IN_SPLITS = tuple(int(s) for s in np.cumsum(IN_SIZES)[:-1])
R_SPLITS = tuple(int(s) for s in np.cumsum(R_SIZES)[:-1])

kernel_name = "hybrid_mlstm_rwkv7_gla_macaron"


def rmsnorm(x, w):
    xf = x.astype(jnp.float32)
    y = xf * lax.rsqrt(jnp.mean(xf * xf, axis=-1, keepdims=True) + EPS)
    return (y * w.astype(jnp.float32)).astype(x.dtype)


def head_rmsnorm(h, w):
    H, d = h.shape[-2:]
    y = h * lax.rsqrt(jnp.mean(h * h, axis=-1, keepdims=True) + EPS)
    return y * w.reshape(H, d)


def head_layernorm(h, w, b, eps):
    H, d = h.shape[-2:]
    mu = jnp.mean(h, axis=-1, keepdims=True)
    var = jnp.mean(jnp.square(h - mu), axis=-1, keepdims=True)
    return (h - mu) * lax.rsqrt(var + eps) * w.reshape(H, d) + b.reshape(H, d)


def swiglu(x, w_in, w_out):
    gate, up = jnp.split(x @ w_in, 2, axis=-1)
    return (jax.nn.silu(gate) * up) @ w_out


def softcap(x):
    return M_GATE_CAP * jnp.tanh(x / M_GATE_CAP)


def token_shift(z):
    return jnp.pad(z, ((0, 0), (1, 0), (0, 0)))[:, :-1]


def causal_dwconv(z, w):
    K, C = w.shape
    return lax.conv_general_dilated(
        z, w.astype(z.dtype)[:, None, :], window_strides=(1,), padding=[(K - 1, 0)],
        dimension_numbers=('NWC', 'WIO', 'NWC'), feature_group_count=C)


def to_chunks(a):
    B, T, H = a.shape[:3]
    a = a.reshape((B, T // CHUNK, CHUNK, H) + a.shape[3:])
    return jnp.moveaxis(a, (1, 3), (0, 2))


def from_chunks(a):
    a = jnp.moveaxis(a, (0, 2), (1, 3))
    B, NC, L, H = a.shape[:4]
    return a.reshape((B, NC * L, H) + a.shape[4:])


def mlstm_chunkwise(q, k, v, ig, lf):
    B, T, H, DK = q.shape
    DV = v.shape[-1]
    mask = jnp.tril(jnp.ones((CHUNK, CHUNK), dtype=bool))

    def step(carry, inp):
        C, n, m = carry
        qc, kc, vc, ic, lfc = inp
        b = jnp.cumsum(lfc, axis=-1)
        log_d = b[..., :, None] - b[..., None, :] + ic[..., None, :]
        log_d = jnp.where(mask, log_d, -jnp.inf)
        log_inter = b + m[..., None]
        m_t = jnp.maximum(log_inter, jnp.max(log_d, axis=-1))
        d = jnp.exp(log_d - m_t[..., None])
        inter = jnp.exp(log_inter - m_t)
        s = jnp.einsum('bhtk,bhsk->bhts', qc, kc) * d
        num = (jnp.einsum('bhts,bhsv->bhtv', s, vc)
               + inter[..., None] * jnp.einsum('bhtk,bhkv->bhtv', qc, C))
        den = jnp.sum(s, axis=-1) + inter * jnp.einsum('bhtk,bhk->bht', qc, n)
        h = num / jnp.maximum(jnp.abs(den), jnp.exp(-m_t))[..., None]
        log_w = b[..., -1:] - b + ic
        m_new = jnp.maximum(b[..., -1] + m, jnp.max(log_w, axis=-1))
        wk = jnp.exp(log_w - m_new[..., None])
        decay = jnp.exp(b[..., -1] + m - m_new)
        C = decay[..., None, None] * C + jnp.einsum('bhs,bhsk,bhsv->bhkv', wk, kc, vc)
        n = decay[..., None] * n + jnp.einsum('bhs,bhsk->bhk', wk, kc)
        return (C, n, m_new), h

    init = (jnp.zeros((B, H, DK, DV), jnp.float32), jnp.zeros((B, H, DK), jnp.float32),
            jnp.zeros((B, H), jnp.float32))
    _, h = lax.scan(step, init, (to_chunks(q), to_chunks(k), to_chunks(v), to_chunks(ig), to_chunks(lf)))
    return from_chunks(h)


def mlstm_branch(q, k, v, o, ig, fg, conv_w, i_bias, f_bias, norm_w):
    B, T, _ = q.shape
    qk = jax.nn.silu(causal_dwconv(jnp.concatenate([q, k], axis=-1), conv_w))
    qk = qk.astype(jnp.float32).reshape(B, T, 2, M_HEADS, M_DK)
    qh = qk[:, :, 0]
    kh = qk[:, :, 1] * (M_DK ** -0.5)
    vh = v.astype(jnp.float32).reshape(B, T, M_HEADS, M_DV)
    log_i = softcap(ig.astype(jnp.float32) + i_bias)
    log_f = jax.nn.log_sigmoid(softcap(fg.astype(jnp.float32) + f_bias))
    h = mlstm_chunkwise(qh, kh, vh, log_i, log_f)
    h = head_rmsnorm(h, norm_w).reshape(B, T, D_BRANCH)
    return h * jax.nn.sigmoid(o.astype(jnp.float32))


def rwkv7_scan(r, w, k, v, a, b):
    B, T, H, N = r.shape
    xs = tuple(jnp.moveaxis(t, 1, 0) for t in (r, w, k, v, a, b))

    def step(S, inp):
        rt, wt, kt, vt, at, bt = inp
        sa = jnp.einsum('bhvk,bhk->bhv', S, at)
        S = (S * wt[:, :, None, :] + sa[..., None] * bt[:, :, None, :]
             + vt[..., None] * kt[:, :, None, :])
        return S, jnp.einsum('bhvk,bhk->bhv', S, rt)

    _, y = lax.scan(step, jnp.zeros((B, H, N, N), jnp.float32), xs)
    return jnp.moveaxis(y, 0, 1)


def rwkv7_branch(z, mu, w0, w2, a0, a2, g2, k_k, k_a, r_k, ln_w, ln_b, v_first, v_res):
    B, T, _ = z.shape
    z = z.astype(jnp.float32)
    z = z + (token_shift(z) - z) * mu
    r, k, v, zw, za, zg = jnp.split(z, R_SPLITS, axis=-1)
    w_log = -jax.nn.softplus(-(w0 + jnp.tanh(zw) @ w2)) - 0.5
    decay = jnp.exp(-jnp.exp(w_log))
    a = jax.nn.sigmoid(a0 + za @ a2)
    g = jax.nn.sigmoid(zg) @ g2
    if v_res is None:
        v_first = v
    else:
        v0, v1, v2 = v_res
        v = v + (v_first - v) * jax.nn.sigmoid(v0 + (v @ v1) @ v2)

    def heads(t):
        return t.reshape(B, T, R_HEADS, R_HEAD_DIM)

    kk = heads(k * k_k)
    kk = kk / jnp.maximum(jnp.linalg.norm(kk, axis=-1, keepdims=True), 1e-12)
    k = k * (1.0 + (a - 1.0) * k_a)
    rh, kh, vh, ah = heads(r), heads(k), heads(v), heads(a)
    y = rwkv7_scan(rh, heads(decay), kh, vh, -kk, kk * ah)
    y = head_layernorm(y, ln_w, ln_b, R_GN_EPS)
    y = y + jnp.sum(rh * kh * r_k.reshape(R_HEADS, R_HEAD_DIM), axis=-1, keepdims=True) * vh
    return y.reshape(B, T, D_BRANCH) * g, v_first


def gla_chunkwise(q, k, v, log_a):
    B, T, H, DK = q.shape
    DV = v.shape[-1]
    mask = jnp.tril(jnp.ones((CHUNK, CHUNK), dtype=bool))

    def step(S, inp):
        qc, kc, vc, gc = inp
        bc = jnp.cumsum(gc, axis=-2)
        diff = bc[..., :, None, :] - bc[..., None, :, :]
        rel = jnp.exp(jnp.where(mask[..., None], diff, -jnp.inf))
        att = jnp.einsum('bhtk,bhsk,bhtsk->bhts', qc, kc, rel)
        o = (jnp.einsum('bhts,bhsv->bhtv', att, vc)
             + jnp.einsum('bhtk,bhkv->bhtv', qc * jnp.exp(bc), S))
        last = bc[..., -1:, :]
        S = (jnp.exp(last[..., 0, :])[..., None] * S
             + jnp.einsum('bhsk,bhsv->bhkv', kc * jnp.exp(last - bc), vc))
        return S, o

    _, o = lax.scan(step, jnp.zeros((B, H, DK, DV), jnp.float32),
                    (to_chunks(q), to_chunks(k), to_chunks(v), to_chunks(log_a)))
    return from_chunks(o)


def gla_branch(q, k, v, zg, go, gk_up, gk_bias, norm_w):
    B, T, _ = q.shape
    f32 = jnp.float32
    qh = q.astype(f32).reshape(B, T, G_HEADS, G_DK) * (G_DK ** -0.5)
    kh = k.astype(f32).reshape(B, T, G_HEADS, G_DK)
    vh = v.astype(f32).reshape(B, T, G_HEADS, G_DV)
    log_a = jax.nn.log_sigmoid(zg.astype(f32) @ gk_up + gk_bias) / G_TAU
    o = gla_chunkwise(qh, kh, vh, log_a.reshape(B, T, G_HEADS, G_DK))
    o = head_rmsnorm(o, norm_w).reshape(B, T, D_BRANCH)
    return o * jax.nn.silu(go.astype(f32))


def hybrid_mixer(h, w_in, m_conv, m_i_bias, m_f_bias, m_norm,
                 r_mu, r_w0, r_w2, r_a0, r_a2, r_g2, r_k_k, r_k_a, r_r_k, r_ln_w, r_ln_b, r_vres,
                 g_gk_up, g_gk_bias, g_norm, w_branch, w_out, v_first):
    B, T, _ = h.shape
    z = h @ w_in
    mq, mk, mv, mo, mi, mf, rz, gq, gk, gv, gz, go, gates = jnp.split(z, IN_SPLITS, axis=-1)
    y_m = mlstm_branch(mq, mk, mv, mo, mi, mf, m_conv, m_i_bias, m_f_bias, m_norm)
    y_r, v_first = rwkv7_branch(rz, r_mu, r_w0, r_w2, r_a0, r_a2, r_g2, r_k_k, r_k_a, r_r_k,
                                r_ln_w, r_ln_b, v_first, r_vres)
    y_g = gla_branch(gq, gk, gv, gz, go, g_gk_up, g_gk_bias, g_norm)
    gate = jax.nn.sigmoid(gates.astype(jnp.float32)).reshape(B, T, N_BRANCH, D_MODEL)
    merged = (gate[:, :, 0] * (y_m.astype(h.dtype) @ w_branch[0])
              + gate[:, :, 1] * (y_r.astype(h.dtype) @ w_branch[1])
              + gate[:, :, 2] * (y_g.astype(h.dtype) @ w_branch[2]))
    return merged.astype(h.dtype) @ w_out, v_first


def setup_inputs(seed: int = 0) -> dict:
    key = jax.random.key(seed)
    ks = iter(jax.random.split(key, 40))

    def nrm(shape, scale):
        return scale * jax.random.normal(next(ks), shape, jnp.float32)

    def gain(shape):
        return 1.0 + nrm(shape, 0.02)

    L = DEPTH
    return {
        "x": nrm((BATCH, SEQ, D_MODEL), 1.0),
        "ffn1_norm": gain((L, D_MODEL)),
        "ffn1_w_in": nrm((L, D_MODEL, 2 * D_FF), D_MODEL ** -0.5),
        "ffn1_w_out": nrm((L, D_FF, D_MODEL), D_FF ** -0.5),
        "mix_norm": gain((L, D_MODEL)),
        "w_in": nrm((L, D_MODEL, N_IN), D_MODEL ** -0.5),
        "m_conv": nrm((L, M_CONV, 2 * M_HEADS * M_DK), M_CONV ** -0.5),
        "m_i_bias": nrm((L, M_HEADS), 0.1),
        "m_f_bias": 3.0 + nrm((L, M_HEADS), 0.5),
        "m_norm": gain((L, D_BRANCH)),
        "r_mu": jax.random.uniform(next(ks), (L, R_COLS), jnp.float32),
        "r_w0": nrm((L, D_BRANCH), 0.5),
        "r_w2": nrm((L, R_W_RANK, D_BRANCH), 0.5 * R_W_RANK ** -0.5),
        "r_a0": nrm((L, D_BRANCH), 0.1),
        "r_a2": nrm((L, R_A_RANK, D_BRANCH), R_A_RANK ** -0.5),
        "r_g2": nrm((L, R_G_RANK, D_BRANCH), R_G_RANK ** -0.5),
        "r_k_k": 1.0 + nrm((L, D_BRANCH), 0.1),
        "r_k_a": 1.0 + nrm((L, D_BRANCH), 0.1),
        "r_r_k": nrm((L, D_BRANCH), 0.1),
        "r_ln_w": gain((L, D_BRANCH)),
        "r_ln_b": nrm((L, D_BRANCH), 0.02),
        "r_v0": nrm((L - 1, D_BRANCH), 0.1),
        "r_v1": nrm((L - 1, D_BRANCH, R_V_RANK), D_BRANCH ** -0.5),
        "r_v2": nrm((L - 1, R_V_RANK, D_BRANCH), R_V_RANK ** -0.5),
        "g_gk_up": nrm((L, G_RANK, G_HEADS * G_DK), G_RANK ** -0.5),
        "g_gk_bias": nrm((L, G_HEADS * G_DK), 0.1),
        "g_norm": gain((L, D_BRANCH)),
        "w_branch": nrm((L, N_BRANCH, D_BRANCH, D_MODEL), D_BRANCH ** -0.5),
        "w_out": nrm((L, D_MODEL, D_MODEL), D_MODEL ** -0.5),
        "ffn2_norm": gain((L, D_MODEL)),
        "ffn2_w_in": nrm((L, D_MODEL, 2 * D_FF), D_MODEL ** -0.5),
        "ffn2_w_out": nrm((L, D_FF, D_MODEL), D_FF ** -0.5),
        "final_norm": gain((D_MODEL,)),
    }


def reference(x, ffn1_norm, ffn1_w_in, ffn1_w_out, mix_norm, w_in, m_conv, m_i_bias, m_f_bias,
              m_norm, r_mu, r_w0, r_w2, r_a0, r_a2, r_g2, r_k_k, r_k_a, r_r_k, r_ln_w, r_ln_b,
              r_v0, r_v1, r_v2, g_gk_up, g_gk_bias, g_norm, w_branch, w_out,
              ffn2_norm, ffn2_w_in, ffn2_w_out, final_norm):
    v_first = None
    for l in range(DEPTH):
        x = x + 0.5 * swiglu(rmsnorm(x, ffn1_norm[l]), ffn1_w_in[l], ffn1_w_out[l])
        h = rmsnorm(x, mix_norm[l])
        r_vres = None if l == 0 else (r_v0[l - 1], r_v1[l - 1], r_v2[l - 1])
        y, v_first = hybrid_mixer(
            h, w_in[l], m_conv[l], m_i_bias[l], m_f_bias[l], m_norm[l],
            r_mu[l], r_w0[l], r_w2[l], r_a0[l], r_a2[l], r_g2[l], r_k_k[l], r_k_a[l], r_r_k[l],
            r_ln_w[l], r_ln_b[l], r_vres,
            g_gk_up[l], g_gk_bias[l], g_norm[l], w_branch[l], w_out[l], v_first)
        x = x + y.astype(x.dtype)
        x = x + 0.5 * swiglu(rmsnorm(x, ffn2_norm[l]), ffn2_w_in[l], ffn2_w_out[l])
    return rmsnorm(x, final_norm)
```

```python
import functools

import jax
import jax.numpy as jnp
from jax import lax
from jax.experimental import pallas as pl
from jax.experimental.pallas import tpu as pltpu

F32 = jnp.float32
BF16 = jnp.bfloat16

D_MODEL = 4096
D_BRANCH = 1024
N_BRANCH = 3
D_FF = 11008
EPS = 1e-6
M_HEADS, M_DV, M_DK, M_CONV, M_GATE_CAP = 4, 256, 128, 4, 15.0
R_HEAD_DIM, R_HEADS = 64, 16
R_W_RANK, R_A_RANK, R_G_RANK, R_V_RANK = 64, 64, 128, 32
R_GN_EPS = 64e-5
G_HEADS, G_DV, G_DK, G_RANK, G_TAU = 4, 256, 128, 16, 16.0

M_SIZES = (M_HEADS * M_DK, M_HEADS * M_DK, D_BRANCH, D_BRANCH, M_HEADS, M_HEADS)
R_SIZES = (D_BRANCH, D_BRANCH, D_BRANCH, R_W_RANK, R_A_RANK, R_G_RANK)
G_SIZES = (G_HEADS * G_DK, G_HEADS * G_DK, D_BRANCH, G_RANK, D_BRANCH)
R_COLS = sum(R_SIZES)
IN_SIZES = M_SIZES + (R_COLS,) + G_SIZES + (N_BRANCH * D_MODEL,)


def _cumsplits(sizes):
    out, acc = [], 0
    for s in sizes[:-1]:
        acc += s
        out.append(acc)
    return tuple(out)


IN_SPLITS = _cumsplits(IN_SIZES)
R_SPLITS = _cumsplits(R_SIZES)

LANE = 128
VMEM_LIMIT = 56 * 1024 * 1024

Z_MV, Z_MO, Z_R, Z_K, Z_V, Z_GV, Z_GO, Z_MQK, Z_GQK = (i * 1024 for i in range(9))
Z_LORA = 9216
Z_MIF = 9472
Z_GZ = 9600
Z_GATES = 9728
Z_COLS = Z_GATES + N_BRANCH * D_MODEL

FF_TN = 256
M_CHUNK = 128
G_CHUNK = 64
R_CHUNK = 64


def _params(sem, vmem=None):
    return pltpu.CompilerParams(dimension_semantics=sem, vmem_limit_bytes=vmem)


def _dotb(a, b):
    return jnp.dot(a.astype(BF16), b.astype(BF16), preferred_element_type=F32)


def _dotb_nt(a, b):
    return lax.dot_general(a.astype(BF16), b.astype(BF16), (((1,), (1,)), ((), ())),
                           preferred_element_type=F32)


def _dotb_tn(a, b):
    return lax.dot_general(a.astype(BF16), b.astype(BF16), (((0,), (0,)), ((), ())),
                           preferred_element_type=F32)


def _dot32(a, b):
    return jnp.dot(a, b, precision=lax.Precision.HIGHEST, preferred_element_type=F32)


def _log_sigmoid(x):
    return jnp.minimum(x, 0.0) - jnp.log1p(jnp.exp(-jnp.abs(x)))


def _sigmoid(x):
    return jax.nn.sigmoid(x)


def _iota(shape, dim):
    return lax.broadcasted_iota(jnp.int32, shape, dim)


def _tril_ones(n):
    return (_iota((n, n), 0) >= _iota((n, n), 1)).astype(F32)


def _rmsnorm_kernel(x_ref, w_ref, o_ref):
    x = x_ref[...]
    ms = jnp.mean(x * x, axis=-1, keepdims=True)
    o_ref[...] = (x * lax.rsqrt(ms + EPS) * w_ref[...]).astype(o_ref.dtype)


def _rmsnorm(x, w, out_dtype, tm=256):
    m, d = x.shape
    return pl.pallas_call(
        _rmsnorm_kernel,
        grid=(m // tm,),
        in_specs=[pl.BlockSpec((tm, d), lambda i: (i, 0)),
                  pl.BlockSpec((1, d), lambda i: (0, 0))],
        out_specs=pl.BlockSpec((tm, d), lambda i: (i, 0)),
        out_shape=jax.ShapeDtypeStruct((m, d), out_dtype),
        compiler_params=_params(("parallel",)),
        name="rmsnorm",
    )(x, w.reshape(1, d))


def _ffn_in_kernel(h_ref, w_ref, o_ref):
    z = jnp.dot(h_ref[...], w_ref[...], preferred_element_type=F32)
    g = z[:, :FF_TN]
    u = z[:, FF_TN:]
    o_ref[...] = (g * _sigmoid(g) * u).astype(o_ref.dtype)


def _ffn_in(h, w_gu, tm):
    m, d = h.shape
    n = w_gu.shape[1] // 2
    return pl.pallas_call(
        _ffn_in_kernel,
        grid=(m // tm, n // FF_TN),
        in_specs=[pl.BlockSpec((tm, d), lambda i, j: (i, 0)),
                  pl.BlockSpec((d, 2 * FF_TN), lambda i, j: (0, j))],
        out_specs=pl.BlockSpec((tm, FF_TN), lambda i, j: (i, j)),
        out_shape=jax.ShapeDtypeStruct((m, n), BF16),
        compiler_params=_params(("parallel", "arbitrary"), VMEM_LIMIT),
        name="ffn_in",
    )(h, w_gu)


def _matmul_kernel(a_ref, w_ref, o_ref):
    o_ref[...] = jnp.dot(a_ref[...], w_ref[...], preferred_element_type=F32).astype(o_ref.dtype)


def _matmul(a, w, tm, tn, out_dtype):
    m, k = a.shape
    n = w.shape[1]
    return pl.pallas_call(
        _matmul_kernel,
        grid=(m // tm, n // tn),
        in_specs=[pl.BlockSpec((tm, k), lambda i, j: (i, 0)),
                  pl.BlockSpec((k, tn), lambda i, j: (0, j))],
        out_specs=pl.BlockSpec((tm, tn), lambda i, j: (i, j)),
        out_shape=jax.ShapeDtypeStruct((m, n), out_dtype),
        compiler_params=_params(("parallel", "arbitrary"), VMEM_LIMIT),
        name="mix_in",
    )(a, w)


def _matmul_res_kernel(scale, a_ref, w_ref, x_ref, o_ref):
    y = jnp.dot(a_ref[...], w_ref[...], preferred_element_type=F32)
    o_ref[...] = x_ref[...] + scale * y


def _matmul_res(a, w, x, scale, tm, tn, name):
    m, k = a.shape
    n = w.shape[1]
    return pl.pallas_call(
        functools.partial(_matmul_res_kernel, scale),
        grid=(m // tm, n // tn),
        in_specs=[pl.BlockSpec((tm, k), lambda i, j: (i, 0)),
                  pl.BlockSpec((k, tn), lambda i, j: (0, j)),
                  pl.BlockSpec((tm, tn), lambda i, j: (i, j))],
        out_specs=pl.BlockSpec((tm, tn), lambda i, j: (i, j)),
        out_shape=jax.ShapeDtypeStruct((m, n), F32),
        compiler_params=_params(("parallel", "arbitrary"), VMEM_LIMIT),
        name=name,
    )(a, w, x)


def _merge_kernel(ym_ref, yr_ref, yg_ref, wb_ref, g0_ref, g1_ref, g2_ref, o_ref):
    acc = _sigmoid(g0_ref[...]) * jnp.dot(ym_ref[...], wb_ref[0], preferred_element_type=F32)
    acc += _sigmoid(g1_ref[...]) * jnp.dot(yr_ref[...], wb_ref[1], preferred_element_type=F32)
    acc += _sigmoid(g2_ref[...]) * jnp.dot(yg_ref[...], wb_ref[2], preferred_element_type=F32)
    o_ref[...] = acc.astype(o_ref.dtype)


def _merge(ym, yr, yg, wb, z2d, tm, tn=512):
    m = ym.shape[0]
    gate_blk = Z_GATES // tn
    per = D_MODEL // tn
    y_spec = pl.BlockSpec((tm, D_BRANCH), lambda i, j: (i, 0))

    def gate_spec(b):
        return pl.BlockSpec((tm, tn), lambda i, j: (i, gate_blk + b * per + j))

    return pl.pallas_call(
        _merge_kernel,
        grid=(m // tm, D_MODEL // tn),
        in_specs=[y_spec, y_spec, y_spec,
                  pl.BlockSpec((N_BRANCH, D_BRANCH, tn), lambda i, j: (0, 0, j)),
                  gate_spec(0), gate_spec(1), gate_spec(2)],
        out_specs=pl.BlockSpec((tm, tn), lambda i, j: (i, j)),
        out_shape=jax.ShapeDtypeStruct((m, D_MODEL), BF16),
        compiler_params=_params(("parallel", "arbitrary"), VMEM_LIMIT),
        name="merge",
    )(ym, yr, yg, wb, z2d, z2d, z2d)


def _mlstm_kernel(qk_ref, v_ref, o_ref, mif_ref, convw_ref, bias_ref, normw_ref, y_ref,
                  qkbuf, cstate, mstate):
    L = M_CHUNK

    @pl.when(pl.program_id(1) == 0)
    def _():
        qkbuf[0:8, :] = jnp.zeros((8, 2 * M_HEADS * M_DK), F32)
        cstate[...] = jnp.zeros_like(cstate)
        mstate[...] = jnp.zeros_like(mstate)

    qkbuf[8:8 + L, :] = qk_ref[...]
    cw = convw_ref[...]
    conv = (cw[0:1] * qkbuf[5:5 + L, :] + cw[1:2] * qkbuf[6:6 + L, :]
            + cw[2:3] * qkbuf[7:7 + L, :] + cw[3:4] * qkbuf[8:8 + L, :])
    qkbuf[0:8, :] = qkbuf[L:L + 8, :]
    qk = conv * _sigmoid(conv)

    capped = M_GATE_CAP * jnp.tanh((mif_ref[...] + bias_ref[...]) * (1.0 / M_GATE_CAP))
    logf = _log_sigmoid(capped)
    bcum = _dot32(_tril_ones(L), logf)
    bcum_t = bcum.T
    capped_t = capped.T

    causal = _iota((L, L), 0) >= _iota((L, L), 1)
    ones_col = (_iota((L, LANE), 1) == 0).astype(F32)
    v_all = v_ref[...]
    o_all = o_ref[...]
    nw = normw_ref[...]

    for h in range(M_HEADS):
        q = qk[:, h * M_DK:(h + 1) * M_DK]
        k = qk[:, (M_HEADS + h) * M_DK:(M_HEADS + h + 1) * M_DK] * (M_DK ** -0.5)
        v_aug = jnp.concatenate([v_all[:, h * M_DV:(h + 1) * M_DV], ones_col], axis=1)
        b_col = bcum[:, M_HEADS + h:M_HEADS + h + 1]
        i_col = capped[:, h:h + 1]
        b_row = bcum_t[M_HEADS + h:M_HEADS + h + 1, :]
        i_row = capped_t[h:h + 1, :]
        m_prev = mstate[h:h + 1, 0:1]
        c_prev = cstate[h]

        log_d = jnp.where(causal, b_col - b_row + i_row, -jnp.inf)
        log_inter = b_col + m_prev
        m_t = jnp.maximum(log_inter, jnp.max(log_d, axis=-1, keepdims=True))
        d = jnp.exp(log_d - m_t)
        inter = jnp.exp(log_inter - m_t)
        s = _dotb_nt(q, k) * d
        numden = _dotb(s, v_aug) + inter * _dotb(q, c_prev)
        num = numden[:, :M_DV]
        den = numden[:, M_DV:M_DV + 1]
        hh = num / jnp.maximum(jnp.abs(den), jnp.exp(-m_t))

        b_last = b_col[L - 1:L, :]
        log_w = b_last - b_col + i_col
        m_new = jnp.maximum(b_last + m_prev, jnp.max(log_w, axis=0, keepdims=True))
        wk = jnp.exp(log_w - m_new)
        decay = jnp.exp(b_last + m_prev - m_new)
        cstate[h] = decay * c_prev + _dotb_tn(k * wk, v_aug)
        mstate[h:h + 1, :] = jnp.broadcast_to(m_new, (1, LANE))

        y = hh * lax.rsqrt(jnp.mean(hh * hh, axis=-1, keepdims=True) + EPS)
        y = y * nw[:, h * M_DV:(h + 1) * M_DV]
        y = y * _sigmoid(o_all[:, h * M_DV:(h + 1) * M_DV])
        y_ref[:, h * M_DV:(h + 1) * M_DV] = y.astype(y_ref.dtype)


def _mlstm(z3, conv_w, gate_bias, norm_w):
    b, t, _ = z3.shape
    L = M_CHUNK

    def zspec(width, col):
        return pl.BlockSpec((None, L, width), lambda i, c: (i, c, col // width))

    def full(shape):
        return pl.BlockSpec(shape, lambda i, c: (0,) * len(shape))

    return pl.pallas_call(
        _mlstm_kernel,
        grid=(b, t // L),
        in_specs=[zspec(1024, Z_MQK), zspec(1024, Z_MV), zspec(1024, Z_MO), zspec(LANE, Z_MIF),
                  full((M_CONV, 1024)), full((1, LANE)), full((1, D_BRANCH))],
        out_specs=pl.BlockSpec((None, L, D_BRANCH), lambda i, c: (i, c, 0)),
        out_shape=jax.ShapeDtypeStruct((b, t, D_BRANCH), BF16),
        scratch_shapes=[pltpu.VMEM((L + 8, 1024), F32),
                        pltpu.VMEM((M_HEADS, M_DK, M_DV + LANE), F32),
                        pltpu.VMEM((8, LANE), F32)],
        compiler_params=_params(("parallel", "arbitrary")),
        name="mlstm",
    )(z3, z3, z3, z3, conv_w, gate_bias, norm_w)


def _gla_kernel(qk_ref, v_ref, go_ref, gz_ref, gkup_ref, gkb_ref, normw_ref, y_ref, sstate):
    L = G_CHUNK

    @pl.when(pl.program_id(1) == 0)
    def _():
        sstate[...] = jnp.zeros_like(sstate)

    x = _dot32(gz_ref[...], gkup_ref[...]) + gkb_ref[...]
    log_a = _log_sigmoid(x) * (1.0 / G_TAU)
    bc = _dot32(_tril_ones(L), log_a)
    last = bc[L - 1:L, :]
    e_pos = jnp.exp(bc)
    e_neg = jnp.exp(-bc)
    e_tail = jnp.exp(last - bc)
    e_last = jnp.exp(last)

    causal = _iota((L, L), 0) >= _iota((L, L), 1)
    qk = qk_ref[...]
    v_all = v_ref[...]
    go = go_ref[...]
    nw = normw_ref[...]

    for h in range(G_HEADS):
        sl = slice(h * G_DK, (h + 1) * G_DK)
        vs = slice(h * G_DV, (h + 1) * G_DV)
        q = qk[:, sl] * (G_DK ** -0.5)
        k = qk[:, G_HEADS * G_DK + h * G_DK:G_HEADS * G_DK + (h + 1) * G_DK]
        v = v_all[:, vs]
        qd = q * e_pos[:, sl]
        att = jnp.where(causal, _dotb_nt(qd, k * e_neg[:, sl]), 0.0)
        s_prev = sstate[h]
        o = _dotb(att, v) + _dotb_nt(qd, s_prev)
        sstate[h] = s_prev * e_last[:, sl] + _dotb_tn(v, k * e_tail[:, sl])

        y = o * lax.rsqrt(jnp.mean(o * o, axis=-1, keepdims=True) + EPS) * nw[:, vs]
        g = go[:, vs]
        y_ref[:, vs] = (y * (g * _sigmoid(g))).astype(y_ref.dtype)


def _gla(z3, gk_up, gk_bias, norm_w):
    b, t, _ = z3.shape
    L = G_CHUNK

    def zspec(width, col):
        return pl.BlockSpec((None, L, width), lambda i, c: (i, c, col // width))

    def full(shape):
        return pl.BlockSpec(shape, lambda i, c: (0,) * len(shape))

    return pl.pallas_call(
        _gla_kernel,
        grid=(b, t // L),
        in_specs=[zspec(1024, Z_GQK), zspec(1024, Z_GV), zspec(1024, Z_GO), zspec(LANE, Z_GZ),
                  full((LANE, G_HEADS * G_DK)), full((1, G_HEADS * G_DK)), full((1, D_BRANCH))],
        out_specs=pl.BlockSpec((None, L, D_BRANCH), lambda i, c: (i, c, 0)),
        out_shape=jax.ShapeDtypeStruct((b, t, D_BRANCH), BF16),
        scratch_shapes=[pltpu.VMEM((G_HEADS, G_DV, G_DK), F32)],
        compiler_params=_params(("parallel", "arbitrary")),
        name="gla",
    )(z3, z3, z3, z3, gk_up, gk_bias, norm_w)


def _pair_sum(x, ones_bd):
    hi = x.astype(BF16)
    lo = (x - hi.astype(F32)).astype(BF16)
    return (jnp.dot(hi, ones_bd, preferred_element_type=F32)
            + jnp.dot(lo, ones_bd, preferred_element_type=F32))


def _rwkv_kernel(has_vres, *refs):
    if has_vres:
        (r_ref, k_ref, v_ref, lora_ref, vfirst_ref, mu_ref, mul_ref, w0_ref, w2_ref, a0_ref, a2_ref,
         g2_ref, kk_ref, ka_ref, rk_ref, lnw_ref, lnb_ref, v0_ref, v1_ref, v2_ref,
         y_ref, zbuf, sstate) = refs
    else:
        (r_ref, k_ref, v_ref, lora_ref, mu_ref, mul_ref, w0_ref, w2_ref, a0_ref, a2_ref,
         g2_ref, kk_ref, ka_ref, rk_ref, lnw_ref, lnb_ref,
         y_ref, vout_ref, zbuf, sstate) = refs
    L = R_CHUNK
    H2 = 2 * R_HEAD_DIM
    n_pairs = R_HEADS // 2

    @pl.when(pl.program_id(1) == 0)
    def _():
        zbuf[0:8, :] = jnp.zeros((8, R_COLS), F32)
        sstate[...] = jnp.zeros_like(sstate)

    zbuf[8:8 + L, 0:1024] = r_ref[...]
    zbuf[8:8 + L, 1024:2048] = k_ref[...]
    zbuf[8:8 + L, 2048:3072] = v_ref[...]
    zbuf[8:8 + L, 3072:R_COLS] = lora_ref[...]
    cur = zbuf[8:8 + L, :]
    prev = zbuf[7:7 + L, :]
    zbuf[0:8, :] = zbuf[L:L + 8, :]
    mu_all = jnp.concatenate([mu_ref[...], mul_ref[...]], axis=1)
    zs = cur + (prev - cur) * mu_all
    r = zs[:, 0:1024]
    k = zs[:, 1024:2048]
    v = zs[:, 2048:3072]
    zwa = zs[:, 3072:3200]
    zg = zs[:, 3200:3328]

    w_log = _log_sigmoid(w0_ref[...] + _dotb(jnp.tanh(zwa), w2_ref[...])) - 0.5
    lw = -jnp.exp(w_log)
    a_lr = _sigmoid(a0_ref[...] + _dotb(zwa, a2_ref[...]))
    g = _dotb(_sigmoid(zg), g2_ref[...])
    if has_vres:
        mix = _sigmoid(v0_ref[...] + _dotb(_dotb(v, v1_ref[...]), v2_ref[...]))
        v = v + (vfirst_ref[...] - v) * mix
    else:
        vout_ref[...] = v

    cum = _dot32(_tril_ones(L), lw)
    cum_last = cum[L - 1:L, :]
    gam = jnp.exp(cum)
    gam_prev = jnp.exp(cum - lw)
    gam_inv = jnp.exp(-cum)
    gam_tail = jnp.exp(cum_last - cum)
    gam_last = jnp.exp(cum_last)

    kk = k * kk_ref[...]
    k_fin = k * (1.0 + (a_lr - 1.0) * ka_ref[...])
    rkr = r * k_fin * rk_ref[...]

    row = _iota((H2, H2), 0)
    col = _iota((H2, H2), 1)
    same_half = (row < R_HEAD_DIM) == (col < R_HEAD_DIM)
    bd = same_half.astype(F32)
    anti = 1.0 - bd
    t_in = row % R_HEAD_DIM
    s_in = col % R_HEAD_DIM
    strict = (s_in < t_in).astype(F32)
    gram_mask = jnp.where(row < R_HEAD_DIM, strict, (s_in <= t_in).astype(F32))
    left = col < R_HEAD_DIM
    lane_lo = (_iota((L, H2), 1) < R_HEAD_DIM)
    eye = (row == col).astype(F32)
    ones_bd = bd.astype(BF16)
    lnw = lnw_ref[...]
    lnb = lnb_ref[...]

    for p in range(n_pairs):
        sl = slice(p * H2, (p + 1) * H2)
        kk_p = kk[:, sl]
        nrm = jnp.sqrt(_pair_sum(kk_p * kk_p, ones_bd))
        kk_n = kk_p / jnp.maximum(nrm, 1e-12)
        a_p = -kk_n
        b_p = kk_n * a_lr[:, sl]
        a_t = a_p * gam_prev[:, sl]
        r_t = r[:, sl] * gam[:, sl]
        b_t = b_p * gam_inv[:, sl]
        k_t = k_fin[:, sl] * gam_inv[:, sl]
        b_h = b_p * gam_tail[:, sl]
        k_h = k_fin[:, sl] * gam_tail[:, sl]
        v_p = v[:, sl]

        x_ar = jnp.concatenate([a_t, r_t], axis=0)
        y_bk = jnp.concatenate([b_t, k_t], axis=0)
        y_kb = jnp.concatenate([k_t, b_t], axis=0)
        lane_lo2 = jnp.concatenate([lane_lo, lane_lo], axis=0)
        g1 = _dotb_nt(jnp.where(lane_lo2, x_ar, 0.0), y_bk) * gram_mask
        g2m = _dotb_nt(jnp.where(lane_lo2, 0.0, x_ar), y_kb) * gram_mask

        n_bd = jnp.concatenate([g1[0:L], g2m[0:L]], axis=0) * bd
        t_inv = eye + n_bd
        pw = n_bd
        for _ in range(5):
            pw = _dotb(pw, pw)
            t_inv = t_inv + _dotb(pw, t_inv)

        ak_sw = jnp.where(lane_lo, g2m[0:L], g1[0:L])
        rk_sw = jnp.where(lane_lo, g2m[L:], g1[L:])
        rb_nat = jnp.where(lane_lo, g1[L:], g2m[L:])
        v2 = jnp.concatenate([v_p, v_p], axis=0)
        m2 = _dotb(jnp.concatenate([ak_sw, rk_sw], axis=0), v2 * anti)

        s_prev = sstate[p]
        m1 = _dotb_nt(x_ar, s_prev)
        rhs = m1[0:L] + m2[0:L]
        ux = _dotb(t_inv, jnp.concatenate([rhs, rhs], axis=0) * bd)
        u = ux[0:L] + ux[L:]
        u2 = jnp.concatenate([u, u], axis=0)
        y = m1[L:] + m2[L:] + _dotb(rb_nat, u2 * bd)
        upd = _dotb_tn(jnp.concatenate([u, v_p], axis=0), jnp.concatenate([b_h, k_h], axis=0))
        sstate[p] = s_prev * gam_last[:, sl] + upd * bd

        mean = _pair_sum(y, ones_bd) * (1.0 / R_HEAD_DIM)
        yc = y - mean
        var = _pair_sum(yc * yc, ones_bd) * (1.0 / R_HEAD_DIM)
        yn = yc * lax.rsqrt(var + R_GN_EPS) * lnw[:, sl] + lnb[:, sl]
        yn = yn + _pair_sum(rkr[:, sl], ones_bd) * v_p
        y_ref[:, sl] = (yn * g[:, sl]).astype(y_ref.dtype)


def _rwkv(z3, v_first, p):
    b, t, _ = z3.shape
    L = R_CHUNK
    has_vres = v_first is not None

    def zspec(width, col):
        return pl.BlockSpec((None, L, width), lambda i, c: (i, c, col // width))

    def full(a):
        return pl.BlockSpec(a.shape, lambda i, c: (0,) * a.ndim)

    seq_spec = pl.BlockSpec((None, L, D_BRANCH), lambda i, c: (i, c, 0))
    names = ["mu", "mul", "w0", "w2", "a0", "a2", "g2", "kk", "ka", "rk", "lnw", "lnb"]
    if has_vres:
        names += ["v0", "v1", "v2"]
    consts = [p[n] for n in names]
    z_specs = [zspec(1024, Z_R), zspec(1024, Z_K), zspec(1024, Z_V), zspec(256, Z_LORA)]
    args = [z3, z3, z3, z3]
    if has_vres:
        z_specs.append(seq_spec)
        args.append(v_first)
    y_shape = jax.ShapeDtypeStruct((b, t, D_BRANCH), BF16)
    if has_vres:
        out_shape, out_specs = y_shape, seq_spec
    else:
        out_shape = (y_shape, jax.ShapeDtypeStruct((b, t, D_BRANCH), F32))
        out_specs = (seq_spec, seq_spec)
    return pl.pallas_call(
        functools.partial(_rwkv_kernel, has_vres),
        grid=(b, t // L),
        in_specs=z_specs + [full(a) for a in consts],
        out_specs=out_specs,
        out_shape=out_shape,
        scratch_shapes=[pltpu.VMEM((L + 8, R_COLS), F32),
                        pltpu.VMEM((R_HEADS // 2, 2 * R_HEAD_DIM, 2 * R_HEAD_DIM), F32)],
        compiler_params=_params(("parallel", "arbitrary")),
        name="rwkv7",
    )(*args, *consts)


def _pad_cols(a, n):
    return jnp.pad(a, ((0, 0), (0, n - a.shape[1])))


def _pad_rows(a, n, before=0):
    return jnp.pad(a, ((before, n - a.shape[0] - before), (0, 0)))


def _prep_w_in(w):
    mq, mk, mv, mo, mi, mf, rz, gq, gk, gv, gz, go, gates = jnp.split(w, IN_SPLITS, axis=1)
    r, k, v, lw, la, lg = jnp.split(rz, R_SPLITS, axis=1)
    cols = [mv, mo, r, k, v, gv, go, mq, mk, gq, gk, lw, la, lg,
            _pad_cols(jnp.concatenate([mi, mf], axis=1), LANE), _pad_cols(gz, LANE), gates]
    return jnp.concatenate(cols, axis=1).astype(BF16)


def _prep_ffn_in(w):
    d = w.shape[0]
    nt = D_FF // FF_TN
    gate = w[:, :D_FF].reshape(d, nt, 1, FF_TN)
    up = w[:, D_FF:].reshape(d, nt, 1, FF_TN)
    return jnp.concatenate([gate, up], axis=2).reshape(d, 2 * D_FF).astype(BF16)


def _row(a):
    return a.reshape(1, -1).astype(F32)


def _pick_tm(m, cap):
    tm = min(m, cap)
    while m % tm:
        tm //= 2
    return tm


def kernel(x, ffn1_norm, ffn1_w_in, ffn1_w_out, mix_norm, w_in, m_conv, m_i_bias, m_f_bias, m_norm, r_mu, r_w0, r_w2, r_a0, r_a2, r_g2, r_k_k, r_k_a, r_r_k, r_ln_w, r_ln_b, r_v0, r_v1, r_v2, g_gk_up, g_gk_bias, g_norm, w_branch, w_out, ffn2_norm, ffn2_w_in, ffn2_w_out, final_norm):
    bsz, seq, d = x.shape
    m = bsz * seq
    depth = w_in.shape[0]
    xs = x.reshape(m, d)
    tm_big = _pick_tm(m, 2048)
    tm_mid = _pick_tm(m, 1024)
    tm_small = _pick_tm(m, 512)

    def ffn(xs, norm_w, w1, w2):
        h = _rmsnorm(xs, norm_w, BF16)
        act = _ffn_in(h, _prep_ffn_in(w1), tm_big)
        return _matmul_res(act, w2.astype(BF16), xs, 0.5, tm_small, 256, "ffn_out")

    v_first = None
    for l in range(depth):
        xs = ffn(xs, ffn1_norm[l], ffn1_w_in[l], ffn1_w_out[l])

        h = _rmsnorm(xs, mix_norm[l], BF16)
        z2 = _matmul(h, _prep_w_in(w_in[l]), tm_mid, 512, F32)
        z3 = z2.reshape(bsz, seq, Z_COLS)

        gate_bias = _pad_cols(jnp.concatenate([m_i_bias[l], m_f_bias[l]]).reshape(1, -1), LANE)
        y_m = _mlstm(z3, m_conv[l], gate_bias, _row(m_norm[l]))

        mu = r_mu[l]
        rp = {
            "mu": _row(mu[:3 * D_BRANCH]), "mul": _row(mu[3 * D_BRANCH:]),
            "w0": _row(r_w0[l]), "w2": _pad_rows(r_w2[l], LANE).astype(BF16),
            "a0": _row(r_a0[l]), "a2": _pad_rows(r_a2[l], LANE, before=R_W_RANK).astype(BF16),
            "g2": r_g2[l].astype(BF16),
            "kk": _row(r_k_k[l]), "ka": _row(r_k_a[l]), "rk": _row(r_r_k[l]),
            "lnw": _row(r_ln_w[l]), "lnb": _row(r_ln_b[l]),
        }
        if l == 0:
            y_r, v_first = _rwkv(z3, None, rp)
        else:
            rp["v0"] = _row(r_v0[l - 1])
            rp["v1"] = _pad_cols(r_v1[l - 1], LANE).astype(BF16)
            rp["v2"] = _pad_rows(r_v2[l - 1], LANE).astype(BF16)
            y_r = _rwkv(z3, v_first, rp)

        y_g = _gla(z3, _pad_rows(g_gk_up[l], LANE), _row(g_gk_bias[l]), _row(g_norm[l]))

        merged = _merge(y_m.reshape(m, D_BRANCH), y_r.reshape(m, D_BRANCH), y_g.reshape(m, D_BRANCH),
                        w_branch[l].astype(BF16), z2, tm_mid)
        xs = _matmul_res(merged, w_out[l].astype(BF16), xs, 1.0, tm_mid, 512, "mix_out")

        xs = ffn(xs, ffn2_norm[l], ffn2_w_in[l], ffn2_w_out[l])

    out = _rmsnorm(xs, final_norm, F32)
    return out.reshape(bsz, seq, d)
```

```python
import functools

import jax
import jax.numpy as jnp
from jax import lax
from jax.experimental import pallas as pl
from jax.experimental.pallas import tpu as pltpu

F32 = jnp.float32
BF16 = jnp.bfloat16

D_MODEL = 4096
D_BRANCH = 1024
N_BRANCH = 3
D_FF = 11008
EPS = 1e-6
M_HEADS, M_DV, M_DK, M_CONV, M_GATE_CAP = 4, 256, 128, 4, 15.0
R_HEAD_DIM, R_HEADS = 64, 16
R_W_RANK, R_A_RANK, R_G_RANK, R_V_RANK = 64, 64, 128, 32
R_GN_EPS = 64e-5
G_HEADS, G_DV, G_DK, G_RANK, G_TAU = 4, 256, 128, 16, 16.0

M_SIZES = (M_HEADS * M_DK, M_HEADS * M_DK, D_BRANCH, D_BRANCH, M_HEADS, M_HEADS)
R_SIZES = (D_BRANCH, D_BRANCH, D_BRANCH, R_W_RANK, R_A_RANK, R_G_RANK)
G_SIZES = (G_HEADS * G_DK, G_HEADS * G_DK, D_BRANCH, G_RANK, D_BRANCH)
R_COLS = sum(R_SIZES)
IN_SIZES = M_SIZES + (R_COLS,) + G_SIZES + (N_BRANCH * D_MODEL,)


def _cumsplits(sizes):
    out, acc = [], 0
    for s in sizes[:-1]:
        acc += s
        out.append(acc)
    return tuple(out)


IN_SPLITS = _cumsplits(IN_SIZES)
R_SPLITS = _cumsplits(R_SIZES)

LANE = 128
VMEM_LIMIT = 56 * 1024 * 1024

Z_MV, Z_MO, Z_R, Z_K, Z_V, Z_GV, Z_GO, Z_MQK, Z_GQK = (i * 1024 for i in range(9))
Z_LORA = 9216
Z_MIF = 9472
Z_GZ = 9600
Z_GATES = 9728
Z_COLS = Z_GATES + N_BRANCH * D_MODEL

FF_TN = 256
M_CHUNK = 128
G_CHUNK = 64
R_CHUNK = 64


def _params(sem, vmem=None):
    return pltpu.CompilerParams(dimension_semantics=sem, vmem_limit_bytes=vmem)


def _dotb(a, b):
    return jnp.dot(a.astype(BF16), b.astype(BF16), preferred_element_type=F32)


def _dotb_nt(a, b):
    return lax.dot_general(a.astype(BF16), b.astype(BF16), (((1,), (1,)), ((), ())),
                           preferred_element_type=F32)


def _dotb_tn(a, b):
    return lax.dot_general(a.astype(BF16), b.astype(BF16), (((0,), (0,)), ((), ())),
                           preferred_element_type=F32)


def _dot32(a, b):
    return jnp.dot(a, b, precision=lax.Precision.HIGHEST, preferred_element_type=F32)


def _log_sigmoid(x):
    return jnp.minimum(x, 0.0) - jnp.log1p(jnp.exp(-jnp.abs(x)))


def _sigmoid(x):
    return jax.nn.sigmoid(x)


def _iota(shape, dim):
    return lax.broadcasted_iota(jnp.int32, shape, dim)


def _tril_ones(n):
    return (_iota((n, n), 0) >= _iota((n, n), 1)).astype(F32)


def _rmsnorm_kernel(x_ref, w_ref, o_ref):
    x = x_ref[...]
    ms = jnp.mean(x * x, axis=-1, keepdims=True)
    o_ref[...] = (x * lax.rsqrt(ms + EPS) * w_ref[...]).astype(o_ref.dtype)


def _rmsnorm(x, w, out_dtype, tm=256):
    m, d = x.shape
    return pl.pallas_call(
        _rmsnorm_kernel,
        grid=(m // tm,),
        in_specs=[pl.BlockSpec((tm, d), lambda i: (i, 0)),
                  pl.BlockSpec((1, d), lambda i: (0, 0))],
        out_specs=pl.BlockSpec((tm, d), lambda i: (i, 0)),
        out_shape=jax.ShapeDtypeStruct((m, d), out_dtype),
        compiler_params=_params(("parallel",)),
        name="rmsnorm",
    )(x, w.reshape(1, d))


def _ffn_in_kernel(h_ref, wg_ref, wu_ref, o_ref, wbuf):
    @pl.when(pl.program_id(1) == 0)
    def _():
        wbuf[:, :FF_TN] = wg_ref[...].astype(BF16)
        wbuf[:, FF_TN:] = wu_ref[...].astype(BF16)

    z = jnp.dot(h_ref[...], wbuf[...], preferred_element_type=F32)
    g = z[:, :FF_TN]
    u = z[:, FF_TN:]
    o_ref[...] = (g * _sigmoid(g) * u).astype(o_ref.dtype)


def _ffn_in(h, w_all, layer, tm):
    m, d = h.shape
    n = w_all.shape[2] // 2
    nt = n // FF_TN
    return pl.pallas_call(
        _ffn_in_kernel,
        grid=(nt, m // tm),
        in_specs=[pl.BlockSpec((tm, d), lambda j, i: (i, 0)),
                  pl.BlockSpec((None, d, FF_TN), lambda j, i: (layer, 0, j)),
                  pl.BlockSpec((None, d, FF_TN), lambda j, i: (layer, 0, j + nt))],
        out_specs=pl.BlockSpec((tm, FF_TN), lambda j, i: (i, j)),
        out_shape=jax.ShapeDtypeStruct((m, n), BF16),
        scratch_shapes=[pltpu.VMEM((d, 2 * FF_TN), BF16)],
        compiler_params=_params(("parallel", "arbitrary"), VMEM_LIMIT),
        name="ffn_in",
    )(h, w_all, w_all)


def _matmul_kernel(a_ref, w_ref, o_ref):
    o_ref[...] = jnp.dot(a_ref[...], w_ref[...], preferred_element_type=F32).astype(o_ref.dtype)


def _matmul(a, w, tm, tn, out_dtype):
    m, k = a.shape
    n = w.shape[1]
    return pl.pallas_call(
        _matmul_kernel,
        grid=(m // tm, n // tn),
        in_specs=[pl.BlockSpec((tm, k), lambda i, j: (i, 0)),
                  pl.BlockSpec((k, tn), lambda i, j: (0, j))],
        out_specs=pl.BlockSpec((tm, tn), lambda i, j: (i, j)),
        out_shape=jax.ShapeDtypeStruct((m, n), out_dtype),
        compiler_params=_params(("parallel", "arbitrary"), VMEM_LIMIT),
        name="mix_in",
    )(a, w)


def _matmul_res_kernel(scale, a_ref, w_ref, x_ref, o_ref):
    y = jnp.dot(a_ref[...], w_ref[...], preferred_element_type=F32)
    o_ref[...] = x_ref[...] + scale * y


def _matmul_res(a, w, x, scale, tm, tn, name):
    m, k = a.shape
    n = w.shape[1]
    return pl.pallas_call(
        functools.partial(_matmul_res_kernel, scale),
        grid=(m // tm, n // tn),
        in_specs=[pl.BlockSpec((tm, k), lambda i, j: (i, 0)),
                  pl.BlockSpec((k, tn), lambda i, j: (0, j)),
                  pl.BlockSpec((tm, tn), lambda i, j: (i, j))],
        out_specs=pl.BlockSpec((tm, tn), lambda i, j: (i, j)),
        out_shape=jax.ShapeDtypeStruct((m, n), F32),
        compiler_params=_params(("parallel", "arbitrary"), VMEM_LIMIT),
        name=name,
    )(a, w, x)


def _merge_kernel(ym_ref, yr_ref, yg_ref, wb_ref, g0_ref, g1_ref, g2_ref, o_ref):
    acc = _sigmoid(g0_ref[...]) * jnp.dot(ym_ref[...], wb_ref[0], preferred_element_type=F32)
    acc += _sigmoid(g1_ref[...]) * jnp.dot(yr_ref[...], wb_ref[1], preferred_element_type=F32)
    acc += _sigmoid(g2_ref[...]) * jnp.dot(yg_ref[...], wb_ref[2], preferred_element_type=F32)
    o_ref[...] = acc.astype(o_ref.dtype)


def _merge(ym, yr, yg, wb, z2d, tm, tn=512):
    m = ym.shape[0]
    gate_blk = Z_GATES // tn
    per = D_MODEL // tn
    y_spec = pl.BlockSpec((tm, D_BRANCH), lambda i, j: (i, 0))

    def gate_spec(b):
        return pl.BlockSpec((tm, tn), lambda i, j: (i, gate_blk + b * per + j))

    return pl.pallas_call(
        _merge_kernel,
        grid=(m // tm, D_MODEL // tn),
        in_specs=[y_spec, y_spec, y_spec,
                  pl.BlockSpec((N_BRANCH, D_BRANCH, tn), lambda i, j: (0, 0, j)),
                  gate_spec(0), gate_spec(1), gate_spec(2)],
        out_specs=pl.BlockSpec((tm, tn), lambda i, j: (i, j)),
        out_shape=jax.ShapeDtypeStruct((m, D_MODEL), BF16),
        compiler_params=_params(("parallel", "arbitrary"), VMEM_LIMIT),
        name="merge",
    )(ym, yr, yg, wb, z2d, z2d, z2d)


def _mlstm_kernel(qk_ref, v_ref, o_ref, mif_ref, convw_ref, bias_ref, normw_ref, y_ref,
                  qkbuf, cstate, mstate):
    L = M_CHUNK

    @pl.when(pl.program_id(1) == 0)
    def _():
        qkbuf[0:8, :] = jnp.zeros((8, 2 * M_HEADS * M_DK), F32)
        cstate[...] = jnp.zeros_like(cstate)
        mstate[...] = jnp.zeros_like(mstate)

    qkbuf[8:8 + L, :] = qk_ref[...]
    cw = convw_ref[...]
    conv = (cw[0:1] * qkbuf[5:5 + L, :] + cw[1:2] * qkbuf[6:6 + L, :]
            + cw[2:3] * qkbuf[7:7 + L, :] + cw[3:4] * qkbuf[8:8 + L, :])
    qkbuf[0:8, :] = qkbuf[L:L + 8, :]
    qk = conv * _sigmoid(conv)

    capped = M_GATE_CAP * jnp.tanh((mif_ref[...] + bias_ref[...]) * (1.0 / M_GATE_CAP))
    logf = _log_sigmoid(capped)
    bcum = _dot32(_tril_ones(L), logf)
    bcum_t = bcum.T
    capped_t = capped.T

    causal = _iota((L, L), 0) >= _iota((L, L), 1)
    ones_col = (_iota((L, LANE), 1) == 0).astype(F32)
    v_all = v_ref[...]
    o_all = o_ref[...]
    nw = normw_ref[...]

    heads = range(M_HEADS)
    qs = [qk[:, h * M_DK:(h + 1) * M_DK] for h in heads]
    ks = [qk[:, (M_HEADS + h) * M_DK:(M_HEADS + h + 1) * M_DK] * (M_DK ** -0.5) for h in heads]
    v_augs = [jnp.concatenate([v_all[:, h * M_DV:(h + 1) * M_DV], ones_col], axis=1) for h in heads]
    c_prevs = [cstate[h] for h in heads]
    qks = [_dotb_nt(qs[h], ks[h]) for h in heads]
    qcs = [_dotb(qs[h], c_prevs[h]) for h in heads]

    ss, inters, m_ts, kws, decays = [], [], [], [], []
    for h in heads:
        b_col = bcum[:, M_HEADS + h:M_HEADS + h + 1]
        i_col = capped[:, h:h + 1]
        b_row = bcum_t[M_HEADS + h:M_HEADS + h + 1, :]
        i_row = capped_t[h:h + 1, :]
        m_prev = mstate[h:h + 1, 0:1]
        log_d = jnp.where(causal, b_col - b_row + i_row, -jnp.inf)
        log_inter = b_col + m_prev
        m_t = jnp.maximum(log_inter, jnp.max(log_d, axis=-1, keepdims=True))
        ss.append(qks[h] * jnp.exp(log_d - m_t))
        inters.append(jnp.exp(log_inter - m_t))
        m_ts.append(m_t)

        b_last = b_col[L - 1:L, :]
        log_w = b_last - b_col + i_col
        m_new = jnp.maximum(b_last + m_prev, jnp.max(log_w, axis=0, keepdims=True))
        kws.append(ks[h] * jnp.exp(log_w - m_new))
        decays.append(jnp.exp(b_last + m_prev - m_new))
        mstate[h:h + 1, :] = jnp.broadcast_to(m_new, (1, LANE))

    svs = [_dotb(ss[h], v_augs[h]) for h in heads]
    c_upd = [_dotb_tn(kws[h], v_augs[h]) for h in heads]
    for h in heads:
        cstate[h] = decays[h] * c_prevs[h] + c_upd[h]
        numden = svs[h] + inters[h] * qcs[h]
        num = numden[:, :M_DV]
        den = numden[:, M_DV:M_DV + 1]
        hh = num / jnp.maximum(jnp.abs(den), jnp.exp(-m_ts[h]))
        y = hh * lax.rsqrt(jnp.mean(hh * hh, axis=-1, keepdims=True) + EPS)
        y = y * nw[:, h * M_DV:(h + 1) * M_DV]
        y = y * _sigmoid(o_all[:, h * M_DV:(h + 1) * M_DV])
        y_ref[:, h * M_DV:(h + 1) * M_DV] = y.astype(y_ref.dtype)


def _mlstm(z3, conv_w, gate_bias, norm_w):
    b, t, _ = z3.shape
    L = M_CHUNK

    def zspec(width, col):
        return pl.BlockSpec((None, L, width), lambda i, c: (i, c, col // width))

    def full(shape):
        return pl.BlockSpec(shape, lambda i, c: (0,) * len(shape))

    return pl.pallas_call(
        _mlstm_kernel,
        grid=(b, t // L),
        in_specs=[zspec(1024, Z_MQK), zspec(1024, Z_MV), zspec(1024, Z_MO), zspec(LANE, Z_MIF),
                  full((M_CONV, 1024)), full((1, LANE)), full((1, D_BRANCH))],
        out_specs=pl.BlockSpec((None, L, D_BRANCH), lambda i, c: (i, c, 0)),
        out_shape=jax.ShapeDtypeStruct((b, t, D_BRANCH), BF16),
        scratch_shapes=[pltpu.VMEM((L + 8, 1024), F32),
                        pltpu.VMEM((M_HEADS, M_DK, M_DV + LANE), F32),
                        pltpu.VMEM((8, LANE), F32)],
        compiler_params=_params(("parallel", "arbitrary")),
        name="mlstm",
    )(z3, z3, z3, z3, conv_w, gate_bias, norm_w)


def _gla_kernel(qk_ref, v_ref, go_ref, gz_ref, gkup_ref, gkb_ref, normw_ref, y_ref, sstate):
    L = G_CHUNK

    @pl.when(pl.program_id(1) == 0)
    def _():
        sstate[...] = jnp.zeros_like(sstate)

    x = _dot32(gz_ref[...], gkup_ref[...]) + gkb_ref[...]
    log_a = _log_sigmoid(x) * (1.0 / G_TAU)
    bc = _dot32(_tril_ones(L), log_a)
    last = bc[L - 1:L, :]
    e_pos = jnp.exp(bc)
    e_neg = jnp.exp(-bc)
    e_tail = jnp.exp(last - bc)
    e_last = jnp.exp(last)

    causal = _iota((L, L), 0) >= _iota((L, L), 1)
    qk = qk_ref[...]
    v_all = v_ref[...]
    go = go_ref[...]
    nw = normw_ref[...]

    heads = range(G_HEADS)
    sls = [slice(h * G_DK, (h + 1) * G_DK) for h in heads]
    vss = [slice(h * G_DV, (h + 1) * G_DV) for h in heads]
    ks = [qk[:, G_HEADS * G_DK + h * G_DK:G_HEADS * G_DK + (h + 1) * G_DK] for h in heads]
    qds = [qk[:, sls[h]] * (G_DK ** -0.5) * e_pos[:, sls[h]] for h in heads]
    s_prevs = [sstate[h] for h in heads]
    atts = [jnp.where(causal, _dotb_nt(qds[h], ks[h] * e_neg[:, sls[h]]), 0.0) for h in heads]
    o_inter = [_dotb_nt(qds[h], s_prevs[h]) for h in heads]
    o_intra = [_dotb(atts[h], v_all[:, vss[h]]) for h in heads]
    s_upd = [_dotb_tn(v_all[:, vss[h]], ks[h] * e_tail[:, sls[h]]) for h in heads]
    for h in heads:
        sstate[h] = s_prevs[h] * e_last[:, sls[h]] + s_upd[h]
        o = o_intra[h] + o_inter[h]
        y = o * lax.rsqrt(jnp.mean(o * o, axis=-1, keepdims=True) + EPS) * nw[:, vss[h]]
        g = go[:, vss[h]]
        y_ref[:, vss[h]] = (y * (g * _sigmoid(g))).astype(y_ref.dtype)


def _gla(z3, gk_up, gk_bias, norm_w):
    b, t, _ = z3.shape
    L = G_CHUNK

    def zspec(width, col):
        return pl.BlockSpec((None, L, width), lambda i, c: (i, c, col // width))

    def full(shape):
        return pl.BlockSpec(shape, lambda i, c: (0,) * len(shape))

    return pl.pallas_call(
        _gla_kernel,
        grid=(b, t // L),
        in_specs=[zspec(1024, Z_GQK), zspec(1024, Z_GV), zspec(1024, Z_GO), zspec(LANE, Z_GZ),
                  full((LANE, G_HEADS * G_DK)), full((1, G_HEADS * G_DK)), full((1, D_BRANCH))],
        out_specs=pl.BlockSpec((None, L, D_BRANCH), lambda i, c: (i, c, 0)),
        out_shape=jax.ShapeDtypeStruct((b, t, D_BRANCH), BF16),
        scratch_shapes=[pltpu.VMEM((G_HEADS, G_DV, G_DK), F32)],
        compiler_params=_params(("parallel", "arbitrary")),
        name="gla",
    )(z3, z3, z3, z3, gk_up, gk_bias, norm_w)


def _rwkv_kernel(has_vres, *refs):
    if has_vres:
        (r_ref, k_ref, v_ref, lora_ref, vfirst_ref, mu_ref, mul_ref, w0_ref, w2_ref, a0_ref, a2_ref,
         g2_ref, kk_ref, ka_ref, rk_ref, lnw_ref, lnb_ref, v0_ref, v1_ref, v2_ref,
         y_ref, zbuf, sstate) = refs
    else:
        (r_ref, k_ref, v_ref, lora_ref, mu_ref, mul_ref, w0_ref, w2_ref, a0_ref, a2_ref,
         g2_ref, kk_ref, ka_ref, rk_ref, lnw_ref, lnb_ref,
         y_ref, vout_ref, zbuf, sstate) = refs
    L = R_CHUNK
    H2 = 2 * R_HEAD_DIM
    n_pairs = R_HEADS // 2

    @pl.when(pl.program_id(1) == 0)
    def _():
        zbuf[0:8, :] = jnp.zeros((8, R_COLS), F32)
        sstate[...] = jnp.zeros_like(sstate)

    zbuf[8:8 + L, 0:1024] = r_ref[...]
    zbuf[8:8 + L, 1024:2048] = k_ref[...]
    zbuf[8:8 + L, 2048:3072] = v_ref[...]
    zbuf[8:8 + L, 3072:R_COLS] = lora_ref[...]
    cur = zbuf[8:8 + L, :]
    prev = zbuf[7:7 + L, :]
    zbuf[0:8, :] = zbuf[L:L + 8, :]
    mu_all = jnp.concatenate([mu_ref[...], mul_ref[...]], axis=1)
    zs = cur + (prev - cur) * mu_all
    r = zs[:, 0:1024]
    k = zs[:, 1024:2048]
    v = zs[:, 2048:3072]
    zwa = zs[:, 3072:3200]
    zg = zs[:, 3200:3328]

    w_log = _log_sigmoid(w0_ref[...] + _dotb(jnp.tanh(zwa), w2_ref[...])) - 0.5
    lw = -jnp.exp(w_log)
    a_lr = _sigmoid(a0_ref[...] + _dotb(zwa, a2_ref[...]))
    g = _dotb(_sigmoid(zg), g2_ref[...])
    if has_vres:
        mix = _sigmoid(v0_ref[...] + _dotb(_dotb(v, v1_ref[...]), v2_ref[...]))
        v = v + (vfirst_ref[...] - v) * mix
    else:
        vout_ref[...] = v

    cum = _dot32(_tril_ones(L), lw)
    cum_last = cum[L - 1:L, :]
    gam = jnp.exp(cum)
    gam_prev = jnp.exp(cum - lw)
    gam_inv = jnp.exp(-cum)
    gam_tail = jnp.exp(cum_last - cum)
    gam_last = jnp.exp(cum_last)

    kk = k * kk_ref[...]
    k_fin = k * (1.0 + (a_lr - 1.0) * ka_ref[...])
    rkr = r * k_fin * rk_ref[...]

    row = _iota((H2, H2), 0)
    col = _iota((H2, H2), 1)
    same_half = (row < R_HEAD_DIM) == (col < R_HEAD_DIM)
    bd = same_half.astype(F32)
    anti = 1.0 - bd
    t_in = row % R_HEAD_DIM
    s_in = col % R_HEAD_DIM
    strict = (s_in < t_in).astype(F32)
    gram_mask = jnp.where(row < R_HEAD_DIM, strict, (s_in <= t_in).astype(F32))
    lane_lo = (_iota((L, H2), 1) < R_HEAD_DIM)
    eye = (row == col).astype(F32)
    ones_bd = bd.astype(BF16)
    lnw = lnw_ref[...]
    lnb = lnb_ref[...]

    pairs = range(n_pairs)
    sls = [slice(p * H2, (p + 1) * H2) for p in pairs]
    lane_lo2 = jnp.concatenate([lane_lo, lane_lo], axis=0)

    def pair_sums(xs):
        his = [x.astype(BF16) for x in xs]
        los = [(x - h.astype(F32)).astype(BF16) for x, h in zip(xs, his)]
        a = [jnp.dot(h, ones_bd, preferred_element_type=F32) for h in his]
        b = [jnp.dot(l, ones_bd, preferred_element_type=F32) for l in los]
        return [x + y for x, y in zip(a, b)]

    v_ps = [v[:, sl] for sl in sls]
    sums = pair_sums([jnp.concatenate([kk[:, sl] * kk[:, sl], rkr[:, sl]], axis=0) for sl in sls])
    x_ars, y_bks, y_kbs, bk_hs = [], [], [], []
    for p, sl in zip(pairs, sls):
        kk_n = kk[:, sl] / jnp.maximum(jnp.sqrt(sums[p][0:L]), 1e-12)
        b_p = kk_n * a_lr[:, sl]
        a_t = -kk_n * gam_prev[:, sl]
        r_t = r[:, sl] * gam[:, sl]
        b_t = b_p * gam_inv[:, sl]
        k_t = k_fin[:, sl] * gam_inv[:, sl]
        x_ars.append(jnp.concatenate([a_t, r_t], axis=0))
        y_bks.append(jnp.concatenate([b_t, k_t], axis=0))
        y_kbs.append(jnp.concatenate([k_t, b_t], axis=0))
        bk_hs.append(jnp.concatenate([b_p * gam_tail[:, sl], k_fin[:, sl] * gam_tail[:, sl]], axis=0))

    g1s = [_dotb_nt(jnp.where(lane_lo2, x_ars[p], 0.0), y_bks[p]) * gram_mask for p in pairs]
    g2s = [_dotb_nt(jnp.where(lane_lo2, 0.0, x_ars[p]), y_kbs[p]) * gram_mask for p in pairs]

    pws = [jnp.concatenate([g1s[p][0:L], g2s[p][0:L]], axis=0) * bd for p in pairs]
    t_invs = [eye + pws[p] for p in pairs]
    for _ in range(5):
        pws = [_dotb(pw, pw) for pw in pws]
        t_upd = [_dotb(pws[p], t_invs[p]) for p in pairs]
        t_invs = [t_invs[p] + t_upd[p] for p in pairs]

    m2s = [_dotb(jnp.where(lane_lo2, g2s[p], g1s[p]),
                 jnp.concatenate([v_ps[p], v_ps[p]], axis=0) * anti) for p in pairs]
    s_prevs = [sstate[p] for p in pairs]
    m1s = [_dotb_nt(x_ars[p], s_prevs[p]) for p in pairs]
    rhss = [m1s[p][0:L] + m2s[p][0:L] for p in pairs]
    uxs = [_dotb(t_invs[p], jnp.concatenate([rhss[p], rhss[p]], axis=0) * bd) for p in pairs]
    us = [ux[0:L] + ux[L:] for ux in uxs]
    rbus = [_dotb(jnp.where(lane_lo, g1s[p][L:], g2s[p][L:]),
                  jnp.concatenate([us[p], us[p]], axis=0) * bd) for p in pairs]
    s_upd = [_dotb_tn(jnp.concatenate([us[p], v_ps[p]], axis=0), bk_hs[p]) for p in pairs]
    for p, sl in zip(pairs, sls):
        sstate[p] = s_prevs[p] * gam_last[:, sl] + s_upd[p] * bd

    ys = [m1s[p][L:] + m2s[p][L:] + rbus[p] for p in pairs]
    means = pair_sums(ys)
    ycs = [ys[p] - means[p] * (1.0 / R_HEAD_DIM) for p in pairs]
    vars_ = pair_sums([yc * yc for yc in ycs])
    for p, sl in zip(pairs, sls):
        yn = ycs[p] * lax.rsqrt(vars_[p] * (1.0 / R_HEAD_DIM) + R_GN_EPS) * lnw[:, sl] + lnb[:, sl]
        yn = yn + sums[p][L:] * v_ps[p]
        y_ref[:, sl] = (yn * g[:, sl]).astype(y_ref.dtype)


def _rwkv(z3, v_first, p):
    b, t, _ = z3.shape
    L = R_CHUNK
    has_vres = v_first is not None

    def zspec(width, col):
        return pl.BlockSpec((None, L, width), lambda i, c: (i, c, col // width))

    def full(a):
        return pl.BlockSpec(a.shape, lambda i, c: (0,) * a.ndim)

    seq_spec = pl.BlockSpec((None, L, D_BRANCH), lambda i, c: (i, c, 0))
    names = ["mu", "mul", "w0", "w2", "a0", "a2", "g2", "kk", "ka", "rk", "lnw", "lnb"]
    if has_vres:
        names += ["v0", "v1", "v2"]
    consts = [p[n] for n in names]
    z_specs = [zspec(1024, Z_R), zspec(1024, Z_K), zspec(1024, Z_V), zspec(256, Z_LORA)]
    args = [z3, z3, z3, z3]
    if has_vres:
        z_specs.append(seq_spec)
        args.append(v_first)
    y_shape = jax.ShapeDtypeStruct((b, t, D_BRANCH), BF16)
    if has_vres:
        out_shape, out_specs = y_shape, seq_spec
    else:
        out_shape = (y_shape, jax.ShapeDtypeStruct((b, t, D_BRANCH), F32))
        out_specs = (seq_spec, seq_spec)
    return pl.pallas_call(
        functools.partial(_rwkv_kernel, has_vres),
        grid=(b, t // L),
        in_specs=z_specs + [full(a) for a in consts],
        out_specs=out_specs,
        out_shape=out_shape,
        scratch_shapes=[pltpu.VMEM((L + 8, R_COLS), F32),
                        pltpu.VMEM((R_HEADS // 2, 2 * R_HEAD_DIM, 2 * R_HEAD_DIM), F32)],
        compiler_params=_params(("parallel", "arbitrary")),
        name="rwkv7",
    )(*args, *consts)


def _pad_cols(a, n):
    return jnp.pad(a, ((0, 0), (0, n - a.shape[1])))


def _pad_rows(a, n, before=0):
    return jnp.pad(a, ((before, n - a.shape[0] - before), (0, 0)))


def _z_tile_table():
    off = dict(zip(("mq", "mk", "mv", "mo", "mi", "mf", "rz", "gq", "gk", "gv", "gz", "go", "gates"),
                   (0,) + IN_SPLITS))
    r0 = off["rz"]
    segs = [(off["mv"], 1024, 1024), (off["mo"], 1024, 1024),
            (r0, 1024, 1024), (r0 + 1024, 1024, 1024), (r0 + 2048, 1024, 1024),
            (off["gv"], 1024, 1024), (off["go"], 1024, 1024),
            (off["mq"], 1024, 1024), (off["gq"], 1024, 1024),
            (r0 + 3072, 256, 256), (off["mi"], 2 * M_HEADS, LANE), (off["gz"], G_RANK, LANE),
            (off["gates"], N_BRANCH * D_MODEL, N_BRANCH * D_MODEL)]
    blk, shift, nvalid = [], [], []
    for src, valid, width in segs:
        for t in range(width // LANE):
            s = src + LANE * t
            blk.append(s // LANE)
            shift.append(s % LANE)
            nvalid.append(max(0, min(LANE, valid - LANE * t)))
    assert len(blk) * LANE == Z_COLS
    return blk, shift, nvalid


def _prep_w_in_kernel(n_src_cols, blk_ref, shift_ref, nvalid_ref, lo_ref, hi_ref, o_ref):
    j = pl.program_id(0)
    n_blocks = pl.cdiv(n_src_cols, LANE)
    b_lo = blk_ref[j]
    b_hi = jnp.minimum(b_lo + 1, n_blocks - 1)
    lane = _iota((1, LANE), 1)
    lo = jnp.where(b_lo * LANE + lane < n_src_cols, lo_ref[...], 0.0)
    hi = jnp.where(b_hi * LANE + lane < n_src_cols, hi_ref[...], 0.0)
    src = jnp.concatenate([lo, hi], axis=1).astype(BF16)
    r = _iota((2 * LANE, LANE), 0)
    c = _iota((2 * LANE, LANE), 1)
    select = ((r == c + shift_ref[j]) & (c < nvalid_ref[j])).astype(BF16)
    o_ref[...] = jnp.dot(src, select, preferred_element_type=F32).astype(o_ref.dtype)


def _prep_w_in(w_all, layer):
    _, d, n_src = w_all.shape
    blk, shift, nvalid = (jnp.asarray(a, jnp.int32) for a in _z_tile_table())
    n_blocks = pl.cdiv(n_src, LANE)
    grid_spec = pltpu.PrefetchScalarGridSpec(
        num_scalar_prefetch=3,
        grid=(Z_COLS // LANE,),
        in_specs=[pl.BlockSpec((None, d, LANE), lambda j, b, s, n: (layer, 0, b[j])),
                  pl.BlockSpec((None, d, LANE),
                               lambda j, b, s, n: (layer, 0, jnp.minimum(b[j] + 1, n_blocks - 1)))],
        out_specs=pl.BlockSpec((d, LANE), lambda j, b, s, n: (0, j)),
    )
    return pl.pallas_call(
        functools.partial(_prep_w_in_kernel, n_src),
        grid_spec=grid_spec,
        out_shape=jax.ShapeDtypeStruct((d, Z_COLS), BF16),
        compiler_params=_params(("parallel",)),
        name="prep_w_in",
    )(blk, shift, nvalid, w_all, w_all)


def _row(a):
    return a.reshape(1, -1).astype(F32)


def _pick_tm(m, cap):
    tm = min(m, cap)
    while m % tm:
        tm //= 2
    return tm


def kernel(x, ffn1_norm, ffn1_w_in, ffn1_w_out, mix_norm, w_in, m_conv, m_i_bias, m_f_bias, m_norm, r_mu, r_w0, r_w2, r_a0, r_a2, r_g2, r_k_k, r_k_a, r_r_k, r_ln_w, r_ln_b, r_v0, r_v1, r_v2, g_gk_up, g_gk_bias, g_norm, w_branch, w_out, ffn2_norm, ffn2_w_in, ffn2_w_out, final_norm):
    bsz, seq, d = x.shape
    m = bsz * seq
    depth = w_in.shape[0]
    xs = x.reshape(m, d)
    tm_mid = _pick_tm(m, 1024)
    tm_small = _pick_tm(m, 512)

    def ffn(xs, norm_w, w1_all, w2, layer):
        h = _rmsnorm(xs, norm_w, BF16)
        act = _ffn_in(h, w1_all, layer, tm_mid)
        return _matmul_res(act, w2.astype(BF16), xs, 0.5, tm_small, 256, "ffn_out")

    v_first = None
    for l in range(depth):
        xs = ffn(xs, ffn1_norm[l], ffn1_w_in, ffn1_w_out[l], l)

        h = _rmsnorm(xs, mix_norm[l], BF16)
        z2 = _matmul(h, _prep_w_in(w_in, l), tm_mid, 512, F32)
        z3 = z2.reshape(bsz, seq, Z_COLS)

        gate_bias = _pad_cols(jnp.concatenate([m_i_bias[l], m_f_bias[l]]).reshape(1, -1), LANE)
        y_m = _mlstm(z3, m_conv[l], gate_bias, _row(m_norm[l]))

        mu = r_mu[l]
        rp = {
            "mu": _row(mu[:3 * D_BRANCH]), "mul": _row(mu[3 * D_BRANCH:]),
            "w0": _row(r_w0[l]), "w2": _pad_rows(r_w2[l], LANE).astype(BF16),
            "a0": _row(r_a0[l]), "a2": _pad_rows(r_a2[l], LANE, before=R_W_RANK).astype(BF16),
            "g2": r_g2[l].astype(BF16),
            "kk": _row(r_k_k[l]), "ka": _row(r_k_a[l]), "rk": _row(r_r_k[l]),
            "lnw": _row(r_ln_w[l]), "lnb": _row(r_ln_b[l]),
        }
        if l == 0:
            y_r, v_first = _rwkv(z3, None, rp)
        else:
            rp["v0"] = _row(r_v0[l - 1])
            rp["v1"] = _pad_cols(r_v1[l - 1], LANE).astype(BF16)
            rp["v2"] = _pad_rows(r_v2[l - 1], LANE).astype(BF16)
            y_r = _rwkv(z3, v_first, rp)

        y_g = _gla(z3, _pad_rows(g_gk_up[l], LANE), _row(g_gk_bias[l]), _row(g_norm[l]))

        merged = _merge(y_m.reshape(m, D_BRANCH), y_r.reshape(m, D_BRANCH), y_g.reshape(m, D_BRANCH),
                        w_branch[l].astype(BF16), z2, tm_mid)
        xs = _matmul_res(merged, w_out[l].astype(BF16), xs, 1.0, tm_mid, 512, "mix_out")

        xs = ffn(xs, ffn2_norm[l], ffn2_w_in, ffn2_w_out[l], l)

    out = _rmsnorm(xs, final_norm, F32)
    return out.reshape(bsz, seq, d)
```

```python
import functools

import jax
import jax.numpy as jnp
from jax import lax
from jax.experimental import pallas as pl
from jax.experimental.pallas import tpu as pltpu

F32 = jnp.float32
BF16 = jnp.bfloat16

D_MODEL = 4096
D_BRANCH = 1024
N_BRANCH = 3
D_FF = 11008
EPS = 1e-6
M_HEADS, M_DV, M_DK, M_CONV, M_GATE_CAP = 4, 256, 128, 4, 15.0
R_HEAD_DIM, R_HEADS = 64, 16
R_W_RANK, R_A_RANK, R_G_RANK, R_V_RANK = 64, 64, 128, 32
R_GN_EPS = 64e-5
G_HEADS, G_DV, G_DK, G_RANK, G_TAU = 4, 256, 128, 16, 16.0

M_SIZES = (M_HEADS * M_DK, M_HEADS * M_DK, D_BRANCH, D_BRANCH, M_HEADS, M_HEADS)
R_SIZES = (D_BRANCH, D_BRANCH, D_BRANCH, R_W_RANK, R_A_RANK, R_G_RANK)
G_SIZES = (G_HEADS * G_DK, G_HEADS * G_DK, D_BRANCH, G_RANK, D_BRANCH)
R_COLS = sum(R_SIZES)
IN_SIZES = M_SIZES + (R_COLS,) + G_SIZES + (N_BRANCH * D_MODEL,)


def _cumsplits(sizes):
    out, acc = [], 0
    for s in sizes[:-1]:
        acc += s
        out.append(acc)
    return tuple(out)


IN_SPLITS = _cumsplits(IN_SIZES)
R_SPLITS = _cumsplits(R_SIZES)

LANE = 128
VMEM_LIMIT = 56 * 1024 * 1024

Z_MV, Z_MO, Z_R, Z_K, Z_V, Z_GV, Z_GO, Z_MQK, Z_GQK = (i * 1024 for i in range(9))
Z_LORA = 9216
Z_MIF = 9472
Z_GZ = 9600
Z_GATES = 9728
Z_COLS = Z_GATES + N_BRANCH * D_MODEL

FF_TN = 256
M_CHUNK = 128
G_CHUNK = 64
G_SUB = 16
R_CHUNK = 64


def _params(sem, vmem=None):
    return pltpu.CompilerParams(dimension_semantics=sem, vmem_limit_bytes=vmem)


def _dotb(a, b):
    return jnp.dot(a.astype(BF16), b.astype(BF16), preferred_element_type=F32)


def _dotb_nt(a, b):
    return lax.dot_general(a.astype(BF16), b.astype(BF16), (((1,), (1,)), ((), ())),
                           preferred_element_type=F32)


def _dotb_tn(a, b):
    return lax.dot_general(a.astype(BF16), b.astype(BF16), (((0,), (0,)), ((), ())),
                           preferred_element_type=F32)


def _dot32(a, b):
    return jnp.dot(a, b, precision=lax.Precision.HIGHEST, preferred_element_type=F32)


def _log_sigmoid(x):
    return jnp.minimum(x, 0.0) - jnp.log1p(jnp.exp(-jnp.abs(x)))


def _sigmoid(x):
    return 0.5 * jnp.tanh(0.5 * x) + 0.5


def _iota(shape, dim):
    return lax.broadcasted_iota(jnp.int32, shape, dim)


def _tril_ones(n):
    return (_iota((n, n), 0) >= _iota((n, n), 1)).astype(F32)


def _rmsnorm_kernel(x_ref, w_ref, o_ref):
    x = x_ref[...]
    ms = jnp.mean(x * x, axis=-1, keepdims=True)
    o_ref[...] = (x * lax.rsqrt(ms + EPS) * w_ref[...]).astype(o_ref.dtype)


def _rmsnorm(x, w, out_dtype, tm=256):
    m, d = x.shape
    return pl.pallas_call(
        _rmsnorm_kernel,
        grid=(m // tm,),
        in_specs=[pl.BlockSpec((tm, d), lambda i: (i, 0)),
                  pl.BlockSpec((1, d), lambda i: (0, 0))],
        out_specs=pl.BlockSpec((tm, d), lambda i: (i, 0)),
        out_shape=jax.ShapeDtypeStruct((m, d), out_dtype),
        compiler_params=_params(("parallel",)),
        name="rmsnorm",
    )(x, w.reshape(1, d))


def _ffn_in_kernel(h_ref, wg_ref, wu_ref, o_ref, wbuf):
    @pl.when(pl.program_id(1) == 0)
    def _():
        wbuf[:, :FF_TN] = wg_ref[...].astype(BF16)
        wbuf[:, FF_TN:] = wu_ref[...].astype(BF16)

    z = jnp.dot(h_ref[...], wbuf[...], preferred_element_type=F32)
    g = z[:, :FF_TN]
    u = z[:, FF_TN:]
    o_ref[...] = (g * _sigmoid(g) * u).astype(o_ref.dtype)


def _ffn_in(h, w_all, layer, tm):
    m, d = h.shape
    n = w_all.shape[2] // 2
    nt = n // FF_TN
    return pl.pallas_call(
        _ffn_in_kernel,
        grid=(nt, m // tm),
        in_specs=[pl.BlockSpec((tm, d), lambda j, i: (i, 0)),
                  pl.BlockSpec((None, d, FF_TN), lambda j, i: (layer, 0, j)),
                  pl.BlockSpec((None, d, FF_TN), lambda j, i: (layer, 0, j + nt))],
        out_specs=pl.BlockSpec((tm, FF_TN), lambda j, i: (i, j)),
        out_shape=jax.ShapeDtypeStruct((m, n), BF16),
        scratch_shapes=[pltpu.VMEM((d, 2 * FF_TN), BF16)],
        compiler_params=_params(("parallel", "arbitrary"), VMEM_LIMIT),
        name="ffn_in",
    )(h, w_all, w_all)


def _matmul_kernel(a_ref, w_ref, o_ref):
    o_ref[...] = jnp.dot(a_ref[...], w_ref[...], preferred_element_type=F32).astype(o_ref.dtype)


def _matmul(a, w, tm, tn, out_dtype):
    m, k = a.shape
    n = w.shape[1]
    return pl.pallas_call(
        _matmul_kernel,
        grid=(m // tm, n // tn),
        in_specs=[pl.BlockSpec((tm, k), lambda i, j: (i, 0)),
                  pl.BlockSpec((k, tn), lambda i, j: (0, j))],
        out_specs=pl.BlockSpec((tm, tn), lambda i, j: (i, j)),
        out_shape=jax.ShapeDtypeStruct((m, n), out_dtype),
        compiler_params=_params(("parallel", "arbitrary"), VMEM_LIMIT),
        name="mix_in",
    )(a, w)


def _matmul_res_kernel(scale, a_ref, w_ref, x_ref, o_ref):
    y = jnp.dot(a_ref[...], w_ref[...], preferred_element_type=F32)
    o_ref[...] = x_ref[...] + scale * y


def _matmul_res(a, w_all, layer, x, scale, tm, tn, name):
    m, k = a.shape
    n = w_all.shape[2]
    return pl.pallas_call(
        functools.partial(_matmul_res_kernel, scale),
        grid=(m // tm, n // tn),
        in_specs=[pl.BlockSpec((tm, k), lambda i, j: (i, 0)),
                  pl.BlockSpec((None, k, tn), lambda i, j: (layer, 0, j)),
                  pl.BlockSpec((tm, tn), lambda i, j: (i, j))],
        out_specs=pl.BlockSpec((tm, tn), lambda i, j: (i, j)),
        out_shape=jax.ShapeDtypeStruct((m, n), F32),
        compiler_params=_params(("parallel", "arbitrary"), VMEM_LIMIT),
        name=name,
    )(a, w_all, x)


def _merge_kernel(ym_ref, yr_ref, yg_ref, wb_ref, g0_ref, g1_ref, g2_ref, o_ref):
    acc = _sigmoid(g0_ref[...]) * jnp.dot(ym_ref[...], wb_ref[0], preferred_element_type=F32)
    acc += _sigmoid(g1_ref[...]) * jnp.dot(yr_ref[...], wb_ref[1], preferred_element_type=F32)
    acc += _sigmoid(g2_ref[...]) * jnp.dot(yg_ref[...], wb_ref[2], preferred_element_type=F32)
    o_ref[...] = acc.astype(o_ref.dtype)


def _merge(ym, yr, yg, wb_all, layer, z2d, tm, tn=512):
    m = ym.shape[0]
    gate_blk = Z_GATES // tn
    per = D_MODEL // tn
    y_spec = pl.BlockSpec((tm, D_BRANCH), lambda i, j: (i, 0))

    def gate_spec(b):
        return pl.BlockSpec((tm, tn), lambda i, j: (i, gate_blk + b * per + j))

    return pl.pallas_call(
        _merge_kernel,
        grid=(m // tm, D_MODEL // tn),
        in_specs=[y_spec, y_spec, y_spec,
                  pl.BlockSpec((None, N_BRANCH, D_BRANCH, tn), lambda i, j: (layer, 0, 0, j)),
                  gate_spec(0), gate_spec(1), gate_spec(2)],
        out_specs=pl.BlockSpec((tm, tn), lambda i, j: (i, j)),
        out_shape=jax.ShapeDtypeStruct((m, D_MODEL), BF16),
        compiler_params=_params(("parallel", "arbitrary"), VMEM_LIMIT),
        name="merge",
    )(ym, yr, yg, wb_all, z2d, z2d, z2d)


def _mlstm_kernel(qk_ref, v_ref, o_ref, mif_ref, convw_ref, bias_ref, normw_ref, y_ref,
                  qkbuf, cstate, mstate):
    L = M_CHUNK

    @pl.when(pl.program_id(1) == 0)
    def _():
        qkbuf[0:8, :] = jnp.zeros((8, 2 * M_HEADS * M_DK), F32)
        cstate[...] = jnp.zeros_like(cstate)
        mstate[...] = jnp.zeros_like(mstate)

    qkbuf[8:8 + L, :] = qk_ref[...]
    cw = convw_ref[...]
    conv = (cw[0:1] * qkbuf[5:5 + L, :] + cw[1:2] * qkbuf[6:6 + L, :]
            + cw[2:3] * qkbuf[7:7 + L, :] + cw[3:4] * qkbuf[8:8 + L, :])
    qkbuf[0:8, :] = qkbuf[L:L + 8, :]
    qk = conv * _sigmoid(conv)

    capped = M_GATE_CAP * jnp.tanh((mif_ref[...] + bias_ref[...]) * (1.0 / M_GATE_CAP))
    logf = _log_sigmoid(capped)
    bcum = _dot32(_tril_ones(L), logf)
    bcum_t = bcum.T
    capped_t = capped.T

    causal = _iota((L, L), 0) >= _iota((L, L), 1)
    ones_col = (_iota((L, LANE), 1) == 0).astype(F32)
    v_all = v_ref[...]
    o_all = o_ref[...]
    nw = normw_ref[...]

    heads = range(M_HEADS)
    qs = [qk[:, h * M_DK:(h + 1) * M_DK] for h in heads]
    ks = [qk[:, (M_HEADS + h) * M_DK:(M_HEADS + h + 1) * M_DK] * (M_DK ** -0.5) for h in heads]
    v_augs = [jnp.concatenate([v_all[:, h * M_DV:(h + 1) * M_DV], ones_col], axis=1) for h in heads]
    c_prevs = [cstate[h] for h in heads]
    qks = [_dotb_nt(qs[h], ks[h]) for h in heads]
    qcs = [_dotb(qs[h], c_prevs[h]) for h in heads]

    ss, inters, m_ts, kws, decays = [], [], [], [], []
    for h in heads:
        b_col = bcum[:, M_HEADS + h:M_HEADS + h + 1]
        i_col = capped[:, h:h + 1]
        b_row = bcum_t[M_HEADS + h:M_HEADS + h + 1, :]
        i_row = capped_t[h:h + 1, :]
        m_prev = mstate[h:h + 1, 0:1]
        log_d = jnp.where(causal, b_col - b_row + i_row, -jnp.inf)
        log_inter = b_col + m_prev
        m_t = jnp.maximum(log_inter, jnp.max(log_d, axis=-1, keepdims=True))
        ss.append(qks[h] * jnp.exp(log_d - m_t))
        inters.append(jnp.exp(log_inter - m_t))
        m_ts.append(m_t)

        b_last = b_col[L - 1:L, :]
        log_w = b_last - b_col + i_col
        m_new = jnp.maximum(b_last + m_prev, jnp.max(log_w, axis=0, keepdims=True))
        kws.append(ks[h] * jnp.exp(log_w - m_new))
        decays.append(jnp.exp(b_last + m_prev - m_new))
        mstate[h:h + 1, :] = jnp.broadcast_to(m_new, (1, LANE))

    svs = [_dotb(ss[h], v_augs[h]) for h in heads]
    c_upd = [_dotb_tn(kws[h], v_augs[h]) for h in heads]
    for h in heads:
        cstate[h] = decays[h] * c_prevs[h] + c_upd[h]
        numden = svs[h] + inters[h] * qcs[h]
        num = numden[:, :M_DV]
        den = numden[:, M_DV:M_DV + 1]
        hh = num / jnp.maximum(jnp.abs(den), jnp.exp(-m_ts[h]))
        y = hh * lax.rsqrt(jnp.mean(hh * hh, axis=-1, keepdims=True) + EPS)
        y = y * nw[:, h * M_DV:(h + 1) * M_DV]
        y = y * _sigmoid(o_all[:, h * M_DV:(h + 1) * M_DV])
        y_ref[:, h * M_DV:(h + 1) * M_DV] = y.astype(y_ref.dtype)


def _mlstm(z3, conv_w, gate_bias, norm_w):
    b, t, _ = z3.shape
    L = M_CHUNK

    def zspec(width, col):
        return pl.BlockSpec((None, L, width), lambda i, c: (i, c, col // width))

    def full(shape):
        return pl.BlockSpec(shape, lambda i, c: (0,) * len(shape))

    return pl.pallas_call(
        _mlstm_kernel,
        grid=(b, t // L),
        in_specs=[zspec(1024, Z_MQK), zspec(1024, Z_MV), zspec(1024, Z_MO), zspec(LANE, Z_MIF),
                  full((M_CONV, 1024)), full((1, LANE)), full((1, D_BRANCH))],
        out_specs=pl.BlockSpec((None, L, D_BRANCH), lambda i, c: (i, c, 0)),
        out_shape=jax.ShapeDtypeStruct((b, t, D_BRANCH), BF16),
        scratch_shapes=[pltpu.VMEM((L + 8, 1024), F32),
                        pltpu.VMEM((M_HEADS, M_DK, M_DV + LANE), F32),
                        pltpu.VMEM((8, LANE), F32)],
        compiler_params=_params(("parallel", "arbitrary")),
        name="mlstm",
    )(z3, z3, z3, z3, conv_w, gate_bias, norm_w)


def _gla_intra(q, k, bc, col_masks):
    L = q.shape[0]
    lane = _iota((G_SUB, L), 1)
    blocks = []
    for c in range(L // G_SUB):
        r0 = c * G_SUB
        qb = q[r0:r0 + G_SUB]
        bcb = bc[r0:r0 + G_SUB]
        acc = jnp.zeros((G_SUB, L), F32)
        if c > 0:
            ref = bc[r0 - 1:r0]
            qc = qb * jnp.exp(bcb - ref)
            kc = k * jnp.exp(jnp.minimum(ref - bc, 0.0))
            acc = jnp.where(lane < r0, _dotb_nt(qc, kc), 0.0)
        for j in range(G_SUB):
            s = r0 + j
            e = jnp.exp(jnp.minimum(bcb - bc[s:s + 1], 0.0))
            col = jnp.sum(qb * e * k[s:s + 1], axis=-1, keepdims=True)
            acc = jnp.where(col_masks[c][j], col, acc)
        blocks.append(acc)
    return jnp.concatenate(blocks, axis=0)


def _gla_kernel(qk_ref, v_ref, go_ref, gz_ref, gkup_ref, gkb_ref, normw_ref, y_ref, sstate):
    L = G_CHUNK

    @pl.when(pl.program_id(1) == 0)
    def _():
        sstate[...] = jnp.zeros_like(sstate)

    x = _dot32(gz_ref[...], gkup_ref[...]) + gkb_ref[...]
    log_a = _log_sigmoid(x) * (1.0 / G_TAU)
    bc = _dot32(_tril_ones(L), log_a)
    last = bc[L - 1:L, :]
    e_pos = jnp.exp(bc)
    e_tail = jnp.exp(last - bc)
    e_last = jnp.exp(last)

    sub_lane = _iota((G_SUB, L), 1)
    sub_row = _iota((G_SUB, L), 0)
    col_masks = [[(sub_lane == c * G_SUB + j) & (sub_row >= j) for j in range(G_SUB)]
                 for c in range(L // G_SUB)]
    qk = qk_ref[...]
    v_all = v_ref[...]
    go = go_ref[...]
    nw = normw_ref[...]

    heads = range(G_HEADS)
    sls = [slice(h * G_DK, (h + 1) * G_DK) for h in heads]
    vss = [slice(h * G_DV, (h + 1) * G_DV) for h in heads]
    ks = [qk[:, G_HEADS * G_DK + h * G_DK:G_HEADS * G_DK + (h + 1) * G_DK] for h in heads]
    qss = [qk[:, sls[h]] * (G_DK ** -0.5) for h in heads]
    qds = [qss[h] * e_pos[:, sls[h]] for h in heads]
    s_prevs = [sstate[h] for h in heads]
    atts = [_gla_intra(qss[h], ks[h], bc[:, sls[h]], col_masks) for h in heads]
    o_inter = [_dotb_nt(qds[h], s_prevs[h]) for h in heads]
    o_intra = [_dotb(atts[h], v_all[:, vss[h]]) for h in heads]
    s_upd = [_dotb_tn(v_all[:, vss[h]], ks[h] * e_tail[:, sls[h]]) for h in heads]
    for h in heads:
        sstate[h] = s_prevs[h] * e_last[:, sls[h]] + s_upd[h]
        o = o_intra[h] + o_inter[h]
        y = o * lax.rsqrt(jnp.mean(o * o, axis=-1, keepdims=True) + EPS) * nw[:, vss[h]]
        g = go[:, vss[h]]
        y_ref[:, vss[h]] = (y * (g * _sigmoid(g))).astype(y_ref.dtype)


def _gla(z3, gk_up, gk_bias, norm_w):
    b, t, _ = z3.shape
    L = G_CHUNK

    def zspec(width, col):
        return pl.BlockSpec((None, L, width), lambda i, c: (i, c, col // width))

    def full(shape):
        return pl.BlockSpec(shape, lambda i, c: (0,) * len(shape))

    return pl.pallas_call(
        _gla_kernel,
        grid=(b, t // L),
        in_specs=[zspec(1024, Z_GQK), zspec(1024, Z_GV), zspec(1024, Z_GO), zspec(LANE, Z_GZ),
                  full((LANE, G_HEADS * G_DK)), full((1, G_HEADS * G_DK)), full((1, D_BRANCH))],
        out_specs=pl.BlockSpec((None, L, D_BRANCH), lambda i, c: (i, c, 0)),
        out_shape=jax.ShapeDtypeStruct((b, t, D_BRANCH), BF16),
        scratch_shapes=[pltpu.VMEM((G_HEADS, G_DV, G_DK), F32)],
        compiler_params=_params(("parallel", "arbitrary")),
        name="gla",
    )(z3, z3, z3, z3, gk_up, gk_bias, norm_w)


def _rwkv_kernel(has_vres, *refs):
    if has_vres:
        (r_ref, k_ref, v_ref, lora_ref, vfirst_ref, mu_ref, mul_ref, w0_ref, w2_ref, a0_ref, a2_ref,
         g2_ref, kk_ref, ka_ref, rk_ref, lnw_ref, lnb_ref, v0_ref, v1_ref, v2_ref,
         y_ref, zbuf, sstate) = refs
    else:
        (r_ref, k_ref, v_ref, lora_ref, mu_ref, mul_ref, w0_ref, w2_ref, a0_ref, a2_ref,
         g2_ref, kk_ref, ka_ref, rk_ref, lnw_ref, lnb_ref,
         y_ref, vout_ref, zbuf, sstate) = refs
    L = R_CHUNK
    H2 = 2 * R_HEAD_DIM
    n_pairs = R_HEADS // 2

    @pl.when(pl.program_id(1) == 0)
    def _():
        zbuf[0:8, :] = jnp.zeros((8, R_COLS), F32)
        sstate[...] = jnp.zeros_like(sstate)

    zbuf[8:8 + L, 0:1024] = r_ref[...]
    zbuf[8:8 + L, 1024:2048] = k_ref[...]
    zbuf[8:8 + L, 2048:3072] = v_ref[...]
    zbuf[8:8 + L, 3072:R_COLS] = lora_ref[...]
    cur = zbuf[8:8 + L, :]
    prev = zbuf[7:7 + L, :]
    zbuf[0:8, :] = zbuf[L:L + 8, :]
    mu_all = jnp.concatenate([mu_ref[...], mul_ref[...]], axis=1)
    zs = cur + (prev - cur) * mu_all
    r = zs[:, 0:1024]
    k = zs[:, 1024:2048]
    v = zs[:, 2048:3072]
    zwa = zs[:, 3072:3200]
    zg = zs[:, 3200:3328]

    w_log = _log_sigmoid(w0_ref[...] + _dotb(jnp.tanh(zwa), w2_ref[...])) - 0.5
    lw = -jnp.exp(w_log)
    a_lr = _sigmoid(a0_ref[...] + _dotb(zwa, a2_ref[...]))
    g = _dotb(_sigmoid(zg), g2_ref[...])
    if has_vres:
        mix = _sigmoid(v0_ref[...] + _dotb(_dotb(v, v1_ref[...]), v2_ref[...]))
        v = v + (vfirst_ref[...] - v) * mix
    else:
        vout_ref[...] = v

    cum = _dot32(_tril_ones(L), lw)
    cum_last = cum[L - 1:L, :]
    gam = jnp.exp(cum)
    gam_prev = jnp.exp(cum - lw)
    gam_inv = jnp.exp(-cum)
    gam_tail = jnp.exp(cum_last - cum)
    gam_last = jnp.exp(cum_last)

    kk = k * kk_ref[...]
    k_fin = k * (1.0 + (a_lr - 1.0) * ka_ref[...])
    rkr = r * k_fin * rk_ref[...]

    row = _iota((H2, H2), 0)
    col = _iota((H2, H2), 1)
    same_half = (row < R_HEAD_DIM) == (col < R_HEAD_DIM)
    bd = same_half.astype(F32)
    anti = 1.0 - bd
    t_in = row % R_HEAD_DIM
    s_in = col % R_HEAD_DIM
    strict = (s_in < t_in).astype(F32)
    gram_mask = jnp.where(row < R_HEAD_DIM, strict, (s_in <= t_in).astype(F32))
    lane_lo = (_iota((L, H2), 1) < R_HEAD_DIM)
    eye = (row == col).astype(F32)
    ones_bd = bd.astype(BF16)
    lnw = lnw_ref[...]
    lnb = lnb_ref[...]

    pairs = range(n_pairs)
    sls = [slice(p * H2, (p + 1) * H2) for p in pairs]
    lane_lo2 = jnp.concatenate([lane_lo, lane_lo], axis=0)

    def pair_sums(xs, split):
        his = [x.astype(BF16) for x in xs]
        a = [jnp.dot(h, ones_bd, preferred_element_type=F32) for h in his]
        if not split:
            return a
        los = [(x - h.astype(F32)).astype(BF16) for x, h in zip(xs, his)]
        b = [jnp.dot(l, ones_bd, preferred_element_type=F32) for l in los]
        return [x + y for x, y in zip(a, b)]

    v_ps = [v[:, sl] for sl in sls]
    sums = pair_sums([jnp.concatenate([kk[:, sl] * kk[:, sl], rkr[:, sl]], axis=0) for sl in sls], True)
    x_ars, y_bks, y_kbs, bk_hs = [], [], [], []
    for p, sl in zip(pairs, sls):
        kk_n = kk[:, sl] / jnp.maximum(jnp.sqrt(sums[p][0:L]), 1e-12)
        b_p = kk_n * a_lr[:, sl]
        a_t = -kk_n * gam_prev[:, sl]
        r_t = r[:, sl] * gam[:, sl]
        b_t = b_p * gam_inv[:, sl]
        k_t = k_fin[:, sl] * gam_inv[:, sl]
        x_ars.append(jnp.concatenate([a_t, r_t], axis=0))
        y_bks.append(jnp.concatenate([b_t, k_t], axis=0))
        y_kbs.append(jnp.concatenate([k_t, b_t], axis=0))
        bk_hs.append(jnp.concatenate([b_p * gam_tail[:, sl], k_fin[:, sl] * gam_tail[:, sl]], axis=0))

    g1s = [_dotb_nt(jnp.where(lane_lo2, x_ars[p], 0.0), y_bks[p]) * gram_mask for p in pairs]
    g2s = [_dotb_nt(jnp.where(lane_lo2, 0.0, x_ars[p]), y_kbs[p]) * gram_mask for p in pairs]

    pws = [jnp.concatenate([g1s[p][0:L], g2s[p][0:L]], axis=0) * bd for p in pairs]
    t_invs = [eye + pws[p] for p in pairs]
    for _ in range(5):
        pws = [_dotb(pw, pw) for pw in pws]
        t_upd = [_dotb(pws[p], t_invs[p]) for p in pairs]
        t_invs = [t_invs[p] + t_upd[p] for p in pairs]

    m2s = [_dotb(jnp.where(lane_lo2, g2s[p], g1s[p]),
                 jnp.concatenate([v_ps[p], v_ps[p]], axis=0) * anti) for p in pairs]
    s_prevs = [sstate[p] for p in pairs]
    m1s = [_dotb_nt(x_ars[p], s_prevs[p]) for p in pairs]
    rhss = [m1s[p][0:L] + m2s[p][0:L] for p in pairs]
    uxs = [_dotb(t_invs[p], jnp.concatenate([rhss[p], rhss[p]], axis=0) * bd) for p in pairs]
    us = [ux[0:L] + ux[L:] for ux in uxs]
    rbus = [_dotb(jnp.where(lane_lo, g1s[p][L:], g2s[p][L:]),
                  jnp.concatenate([us[p], us[p]], axis=0) * bd) for p in pairs]
    s_upd = [_dotb_tn(jnp.concatenate([us[p], v_ps[p]], axis=0), bk_hs[p]) for p in pairs]
    for p, sl in zip(pairs, sls):
        sstate[p] = s_prevs[p] * gam_last[:, sl] + s_upd[p] * bd

    ys = [m1s[p][L:] + m2s[p][L:] + rbus[p] for p in pairs]
    means = pair_sums(ys, False)
    ycs = [ys[p] - means[p] * (1.0 / R_HEAD_DIM) for p in pairs]
    vars_ = pair_sums([yc * yc for yc in ycs], False)
    for p, sl in zip(pairs, sls):
        yn = ycs[p] * lax.rsqrt(vars_[p] * (1.0 / R_HEAD_DIM) + R_GN_EPS) * lnw[:, sl] + lnb[:, sl]
        yn = yn + sums[p][L:] * v_ps[p]
        y_ref[:, sl] = (yn * g[:, sl]).astype(y_ref.dtype)


def _rwkv(z3, v_first, p):
    b, t, _ = z3.shape
    L = R_CHUNK
    has_vres = v_first is not None

    def zspec(width, col):
        return pl.BlockSpec((None, L, width), lambda i, c: (i, c, col // width))

    def full(a):
        return pl.BlockSpec(a.shape, lambda i, c: (0,) * a.ndim)

    seq_spec = pl.BlockSpec((None, L, D_BRANCH), lambda i, c: (i, c, 0))
    names = ["mu", "mul", "w0", "w2", "a0", "a2", "g2", "kk", "ka", "rk", "lnw", "lnb"]
    if has_vres:
        names += ["v0", "v1", "v2"]
    consts = [p[n] for n in names]
    z_specs = [zspec(1024, Z_R), zspec(1024, Z_K), zspec(1024, Z_V), zspec(256, Z_LORA)]
    args = [z3, z3, z3, z3]
    if has_vres:
        z_specs.append(seq_spec)
        args.append(v_first)
    y_shape = jax.ShapeDtypeStruct((b, t, D_BRANCH), BF16)
    if has_vres:
        out_shape, out_specs = y_shape, seq_spec
    else:
        out_shape = (y_shape, jax.ShapeDtypeStruct((b, t, D_BRANCH), F32))
        out_specs = (seq_spec, seq_spec)
    return pl.pallas_call(
        functools.partial(_rwkv_kernel, has_vres),
        grid=(b, t // L),
        in_specs=z_specs + [full(a) for a in consts],
        out_specs=out_specs,
        out_shape=out_shape,
        scratch_shapes=[pltpu.VMEM((L + 8, R_COLS), F32),
                        pltpu.VMEM((R_HEADS // 2, 2 * R_HEAD_DIM, 2 * R_HEAD_DIM), F32)],
        compiler_params=_params(("parallel", "arbitrary")),
        name="rwkv7",
    )(*args, *consts)


def _pad_cols(a, n):
    return jnp.pad(a, ((0, 0), (0, n - a.shape[1])))


def _pad_rows(a, n, before=0):
    return jnp.pad(a, ((before, n - a.shape[0] - before), (0, 0)))


def _z_tile_table():
    off = dict(zip(("mq", "mk", "mv", "mo", "mi", "mf", "rz", "gq", "gk", "gv", "gz", "go", "gates"),
                   (0,) + IN_SPLITS))
    r0 = off["rz"]
    segs = [(off["mv"], 1024, 1024), (off["mo"], 1024, 1024),
            (r0, 1024, 1024), (r0 + 1024, 1024, 1024), (r0 + 2048, 1024, 1024),
            (off["gv"], 1024, 1024), (off["go"], 1024, 1024),
            (off["mq"], 1024, 1024), (off["gq"], 1024, 1024),
            (r0 + 3072, 256, 256), (off["mi"], 2 * M_HEADS, LANE), (off["gz"], G_RANK, LANE),
            (off["gates"], N_BRANCH * D_MODEL, N_BRANCH * D_MODEL)]
    start, nvalid = [], []
    for src, valid, width in segs:
        for t in range(width // LANE):
            assert (src + LANE * t) % 8 == 0
            start.append(src + LANE * t)
            nvalid.append(max(0, min(LANE, valid - LANE * t)))
    assert len(start) * LANE == Z_COLS
    return start, nvalid


def _prep_w_in_kernel(start_ref, nvalid_ref, src_ref, o_ref):
    del start_ref
    r = _iota((LANE, LANE), 0)
    c = _iota((LANE, LANE), 1)
    select = ((r == c) & (c < nvalid_ref[pl.program_id(0)])).astype(BF16)
    o_ref[...] = _dotb_tn(src_ref[...], select).astype(o_ref.dtype)


def _prep_w_in(w_all_t, layer):
    n_layers, n_src, d = w_all_t.shape
    start, nvalid = (jnp.asarray(a, jnp.int32) for a in _z_tile_table())
    grid_spec = pltpu.PrefetchScalarGridSpec(
        num_scalar_prefetch=2,
        grid=(Z_COLS // LANE,),
        in_specs=[pl.BlockSpec((pl.Element(LANE), pl.Element(d)), lambda j, s, n: (s[j] * 8, 0))],
        out_specs=pl.BlockSpec((d, LANE), lambda j, s, n: (0, j)),
    )
    return pl.pallas_call(
        _prep_w_in_kernel,
        grid_spec=grid_spec,
        out_shape=jax.ShapeDtypeStruct((d, Z_COLS), BF16),
        compiler_params=_params(("parallel",)),
        name="prep_w_in",
    )((start + layer * n_src) // 8, nvalid, w_all_t.reshape(n_layers * n_src, d))


def _row(a):
    return a.reshape(1, -1).astype(F32)


def _pick_tm(m, cap):
    tm = min(m, cap)
    while m % tm:
        tm //= 2
    return tm


def kernel(x, ffn1_norm, ffn1_w_in, ffn1_w_out, mix_norm, w_in, m_conv, m_i_bias, m_f_bias, m_norm, r_mu, r_w0, r_w2, r_a0, r_a2, r_g2, r_k_k, r_k_a, r_r_k, r_ln_w, r_ln_b, r_v0, r_v1, r_v2, g_gk_up, g_gk_bias, g_norm, w_branch, w_out, ffn2_norm, ffn2_w_in, ffn2_w_out, final_norm):
    bsz, seq, d = x.shape
    m = bsz * seq
    depth = w_in.shape[0]
    xs = x.reshape(m, d)
    tm_mid = _pick_tm(m, 1024)
    tm_small = _pick_tm(m, 512)

    def ffn(xs, norm_w, w1_all, w2_all, layer):
        h = _rmsnorm(xs, norm_w, BF16)
        act = _ffn_in(h, w1_all, layer, tm_mid)
        return _matmul_res(act, w2_all, layer, xs, 0.5, tm_small, 512, "ffn_out")

    ffn1_w_out_b = ffn1_w_out.astype(BF16)
    ffn2_w_out_b = ffn2_w_out.astype(BF16)
    w_branch_b = w_branch.astype(BF16)
    w_out_b = w_out.astype(BF16)
    w_in_t = jnp.swapaxes(w_in, 1, 2)

    v_first = None
    for l in range(depth):
        xs = ffn(xs, ffn1_norm[l], ffn1_w_in, ffn1_w_out_b, l)

        h = _rmsnorm(xs, mix_norm[l], BF16)
        z2 = _matmul(h, _prep_w_in(w_in_t, l), tm_mid, 512, F32)
        z3 = z2.reshape(bsz, seq, Z_COLS)

        gate_bias = _pad_cols(jnp.concatenate([m_i_bias[l], m_f_bias[l]]).reshape(1, -1), LANE)
        y_m = _mlstm(z3, m_conv[l], gate_bias, _row(m_norm[l]))

        mu = r_mu[l]
        rp = {
            "mu": _row(mu[:3 * D_BRANCH]), "mul": _row(mu[3 * D_BRANCH:]),
            "w0": _row(r_w0[l]), "w2": _pad_rows(r_w2[l], LANE).astype(BF16),
            "a0": _row(r_a0[l]), "a2": _pad_rows(r_a2[l], LANE, before=R_W_RANK).astype(BF16),
            "g2": r_g2[l].astype(BF16),
            "kk": _row(r_k_k[l]), "ka": _row(r_k_a[l]), "rk": _row(r_r_k[l]),
            "lnw": _row(r_ln_w[l]), "lnb": _row(r_ln_b[l]),
        }
        if l == 0:
            y_r, v_first = _rwkv(z3, None, rp)
        else:
            rp["v0"] = _row(r_v0[l - 1])
            rp["v1"] = _pad_cols(r_v1[l - 1], LANE).astype(BF16)
            rp["v2"] = _pad_rows(r_v2[l - 1], LANE).astype(BF16)
            y_r = _rwkv(z3, v_first, rp)

        y_g = _gla(z3, _pad_rows(g_gk_up[l], LANE), _row(g_gk_bias[l]), _row(g_norm[l]))

        merged = _merge(y_m.reshape(m, D_BRANCH), y_r.reshape(m, D_BRANCH), y_g.reshape(m, D_BRANCH),
                        w_branch_b, l, z2, tm_mid)
        xs = _matmul_res(merged, w_out_b, l, xs, 1.0, tm_mid, 512, "mix_out")

        xs = ffn(xs, ffn2_norm[l], ffn2_w_in, ffn2_w_out_b, l)

    out = _rmsnorm(xs, final_norm, F32)
    return out.reshape(bsz, seq, d)
```

```python
import functools

import jax
import jax.numpy as jnp
from jax import lax
from jax.experimental import pallas as pl
from jax.experimental.pallas import tpu as pltpu

F32 = jnp.float32
BF16 = jnp.bfloat16

D_MODEL = 4096
D_BRANCH = 1024
N_BRANCH = 3
D_FF = 11008
EPS = 1e-6
M_HEADS, M_DV, M_DK, M_CONV, M_GATE_CAP = 4, 256, 128, 4, 15.0
R_HEAD_DIM, R_HEADS = 64, 16
R_W_RANK, R_A_RANK, R_G_RANK, R_V_RANK = 64, 64, 128, 32
R_GN_EPS = 64e-5
G_HEADS, G_DV, G_DK, G_RANK, G_TAU = 4, 256, 128, 16, 16.0

M_SIZES = (M_HEADS * M_DK, M_HEADS * M_DK, D_BRANCH, D_BRANCH, M_HEADS, M_HEADS)
R_SIZES = (D_BRANCH, D_BRANCH, D_BRANCH, R_W_RANK, R_A_RANK, R_G_RANK)
G_SIZES = (G_HEADS * G_DK, G_HEADS * G_DK, D_BRANCH, G_RANK, D_BRANCH)
R_COLS = sum(R_SIZES)
IN_SIZES = M_SIZES + (R_COLS,) + G_SIZES + (N_BRANCH * D_MODEL,)


def _cumsplits(sizes):
    out, acc = [], 0
    for s in sizes[:-1]:
        acc += s
        out.append(acc)
    return tuple(out)


IN_SPLITS = _cumsplits(IN_SIZES)
R_SPLITS = _cumsplits(R_SIZES)

LANE = 128
VMEM_LIMIT = 56 * 1024 * 1024

Z_MV, Z_MO, Z_R, Z_K, Z_V, Z_GV, Z_GO, Z_MQK, Z_GQK = (i * 1024 for i in range(9))
Z_LORA = 9216
Z_MIF = 9472
Z_GZ = 9600
Z_GATES = 9728
Z_COLS = Z_GATES + N_BRANCH * D_MODEL

FF_TN = 256
M_CHUNK = 128
G_CHUNK = 64
G_SUB = 16
R_CHUNK = 64


def _params(sem, vmem=None):
    return pltpu.CompilerParams(dimension_semantics=sem, vmem_limit_bytes=vmem)


def _dotb(a, b):
    return jnp.dot(a.astype(BF16), b.astype(BF16), preferred_element_type=F32)


def _dotb_nt(a, b):
    return lax.dot_general(a.astype(BF16), b.astype(BF16), (((1,), (1,)), ((), ())),
                           preferred_element_type=F32)


def _dotb_tn(a, b):
    return lax.dot_general(a.astype(BF16), b.astype(BF16), (((0,), (0,)), ((), ())),
                           preferred_element_type=F32)


def _dot32(a, b):
    return jnp.dot(a, b, precision=lax.Precision.HIGHEST, preferred_element_type=F32)


def _log_sigmoid(x):
    return jnp.minimum(x, 0.0) - jnp.log1p(jnp.exp(-jnp.abs(x)))


def _sigmoid(x):
    return 0.5 * jnp.tanh(0.5 * x) + 0.5


def _iota(shape, dim):
    return lax.broadcasted_iota(jnp.int32, shape, dim)


def _tril_ones(n):
    return (_iota((n, n), 0) >= _iota((n, n), 1)).astype(F32)


def _rmsnorm_kernel(x_ref, w_ref, o_ref):
    x = x_ref[...]
    ms = jnp.mean(x * x, axis=-1, keepdims=True)
    o_ref[...] = (x * lax.rsqrt(ms + EPS) * w_ref[...]).astype(o_ref.dtype)


def _rmsnorm(x, w, out_dtype, tm=256):
    m, d = x.shape
    return pl.pallas_call(
        _rmsnorm_kernel,
        grid=(m // tm,),
        in_specs=[pl.BlockSpec((tm, d), lambda i: (i, 0)),
                  pl.BlockSpec((1, d), lambda i: (0, 0))],
        out_specs=pl.BlockSpec((tm, d), lambda i: (i, 0)),
        out_shape=jax.ShapeDtypeStruct((m, d), out_dtype),
        compiler_params=_params(("parallel",)),
        name="rmsnorm",
    )(x, w.reshape(1, d))


def _ffn_in_kernel(h_ref, wg_ref, wu_ref, o_ref, wbuf):
    @pl.when(pl.program_id(1) == 0)
    def _():
        wbuf[:, :FF_TN] = wg_ref[...].astype(BF16)
        wbuf[:, FF_TN:] = wu_ref[...].astype(BF16)

    z = jnp.dot(h_ref[...], wbuf[...], preferred_element_type=F32)
    g = z[:, :FF_TN]
    u = z[:, FF_TN:]
    o_ref[...] = (g * _sigmoid(g) * u).astype(o_ref.dtype)


def _ffn_in(h, w_all, layer, tm):
    m, d = h.shape
    n = w_all.shape[2] // 2
    nt = n // FF_TN
    return pl.pallas_call(
        _ffn_in_kernel,
        grid=(nt, m // tm),
        in_specs=[pl.BlockSpec((tm, d), lambda j, i: (i, 0)),
                  pl.BlockSpec((None, d, FF_TN), lambda j, i: (layer, 0, j), pipeline_mode=pl.Buffered(1)),
                  pl.BlockSpec((None, d, FF_TN), lambda j, i: (layer, 0, j + nt),
                               pipeline_mode=pl.Buffered(1))],
        out_specs=pl.BlockSpec((tm, FF_TN), lambda j, i: (i, j)),
        out_shape=jax.ShapeDtypeStruct((m, n), BF16),
        scratch_shapes=[pltpu.VMEM((d, 2 * FF_TN), BF16)],
        compiler_params=_params(("parallel", "arbitrary"), VMEM_LIMIT),
        name="ffn_in",
    )(h, w_all, w_all)


def _matmul_kernel(a_ref, w_ref, o_ref):
    o_ref[...] = jnp.dot(a_ref[...], w_ref[...], preferred_element_type=F32).astype(o_ref.dtype)


def _matmul(a, w, tm, tn, out_dtype):
    m, k = a.shape
    n = w.shape[1]
    return pl.pallas_call(
        _matmul_kernel,
        grid=(m // tm, n // tn),
        in_specs=[pl.BlockSpec((tm, k), lambda i, j: (i, 0)),
                  pl.BlockSpec((k, tn), lambda i, j: (0, j))],
        out_specs=pl.BlockSpec((tm, tn), lambda i, j: (i, j)),
        out_shape=jax.ShapeDtypeStruct((m, n), out_dtype),
        compiler_params=_params(("parallel", "arbitrary"), VMEM_LIMIT),
        name="mix_in",
    )(a, w)


def _matmul_res_kernel(scale, a_ref, w_ref, x_ref, o_ref):
    y = jnp.dot(a_ref[...], w_ref[...], preferred_element_type=F32)
    o_ref[...] = x_ref[...] + scale * y


def _matmul_res(a, w_all, layer, x, scale, tm, tn, name):
    m, k = a.shape
    n = w_all.shape[2]
    return pl.pallas_call(
        functools.partial(_matmul_res_kernel, scale),
        grid=(m // tm, n // tn),
        in_specs=[pl.BlockSpec((tm, k), lambda i, j: (i, 0)),
                  pl.BlockSpec((None, k, tn), lambda i, j: (layer, 0, j)),
                  pl.BlockSpec((tm, tn), lambda i, j: (i, j))],
        out_specs=pl.BlockSpec((tm, tn), lambda i, j: (i, j)),
        out_shape=jax.ShapeDtypeStruct((m, n), F32),
        compiler_params=_params(("parallel", "arbitrary"), VMEM_LIMIT),
        name=name,
    )(a, w_all, x)


def _merge_kernel(ym_ref, yr_ref, yg_ref, wb_ref, g0_ref, g1_ref, g2_ref, o_ref):
    acc = _sigmoid(g0_ref[...]) * jnp.dot(ym_ref[...], wb_ref[0], preferred_element_type=F32)
    acc += _sigmoid(g1_ref[...]) * jnp.dot(yr_ref[...], wb_ref[1], preferred_element_type=F32)
    acc += _sigmoid(g2_ref[...]) * jnp.dot(yg_ref[...], wb_ref[2], preferred_element_type=F32)
    o_ref[...] = acc.astype(o_ref.dtype)


def _merge(ym, yr, yg, wb_all, layer, z2d, tm, tn=512):
    m = ym.shape[0]
    gate_blk = Z_GATES // tn
    per = D_MODEL // tn
    y_spec = pl.BlockSpec((tm, D_BRANCH), lambda i, j: (i, 0))

    def gate_spec(b):
        return pl.BlockSpec((tm, tn), lambda i, j: (i, gate_blk + b * per + j))

    return pl.pallas_call(
        _merge_kernel,
        grid=(m // tm, D_MODEL // tn),
        in_specs=[y_spec, y_spec, y_spec,
                  pl.BlockSpec((None, N_BRANCH, D_BRANCH, tn), lambda i, j: (layer, 0, 0, j)),
                  gate_spec(0), gate_spec(1), gate_spec(2)],
        out_specs=pl.BlockSpec((tm, tn), lambda i, j: (i, j)),
        out_shape=jax.ShapeDtypeStruct((m, D_MODEL), BF16),
        compiler_params=_params(("parallel", "arbitrary"), VMEM_LIMIT),
        name="merge",
    )(ym, yr, yg, wb_all, z2d, z2d, z2d)


def _mlstm_kernel(qk_ref, v_ref, o_ref, mif_ref, convw_ref, bias_ref, normw_ref, y_ref,
                  qkbuf, cstate, mstate):
    L = M_CHUNK

    @pl.when(pl.program_id(1) == 0)
    def _():
        qkbuf[0:8, :] = jnp.zeros((8, 2 * M_HEADS * M_DK), F32)
        cstate[...] = jnp.zeros_like(cstate)
        mstate[...] = jnp.zeros_like(mstate)

    qkbuf[8:8 + L, :] = qk_ref[...]
    cw = convw_ref[...]
    conv = (cw[0:1] * qkbuf[5:5 + L, :] + cw[1:2] * qkbuf[6:6 + L, :]
            + cw[2:3] * qkbuf[7:7 + L, :] + cw[3:4] * qkbuf[8:8 + L, :])
    qkbuf[0:8, :] = qkbuf[L:L + 8, :]
    qk = conv * _sigmoid(conv)

    capped = M_GATE_CAP * jnp.tanh((mif_ref[...] + bias_ref[...]) * (1.0 / M_GATE_CAP))
    logf = _log_sigmoid(capped)
    bcum = _dot32(_tril_ones(L), logf)
    bcum_t = bcum.T
    capped_t = capped.T

    causal = _iota((L, L), 0) >= _iota((L, L), 1)
    ones_col = (_iota((L, LANE), 1) == 0).astype(F32)
    v_all = v_ref[...]
    o_all = o_ref[...]
    nw = normw_ref[...]

    heads = range(M_HEADS)
    qs = [qk[:, h * M_DK:(h + 1) * M_DK] for h in heads]
    ks = [qk[:, (M_HEADS + h) * M_DK:(M_HEADS + h + 1) * M_DK] * (M_DK ** -0.5) for h in heads]
    v_augs = [jnp.concatenate([v_all[:, h * M_DV:(h + 1) * M_DV], ones_col], axis=1) for h in heads]
    c_prevs = [cstate[h] for h in heads]
    qks = [_dotb_nt(qs[h], ks[h]) for h in heads]
    qcs = [_dotb(qs[h], c_prevs[h]) for h in heads]

    ss, inters, m_ts, kws, decays = [], [], [], [], []
    for h in heads:
        b_col = bcum[:, M_HEADS + h:M_HEADS + h + 1]
        i_col = capped[:, h:h + 1]
        b_row = bcum_t[M_HEADS + h:M_HEADS + h + 1, :]
        i_row = capped_t[h:h + 1, :]
        m_prev = mstate[h:h + 1, 0:1]
        log_d = jnp.where(causal, b_col - b_row + i_row, -jnp.inf)
        log_inter = b_col + m_prev
        m_t = jnp.maximum(log_inter, jnp.max(log_d, axis=-1, keepdims=True))
        ss.append(qks[h] * jnp.exp(log_d - m_t))
        inters.append(jnp.exp(log_inter - m_t))
        m_ts.append(m_t)

        b_last = b_col[L - 1:L, :]
        log_w = b_last - b_col + i_col
        m_new = jnp.maximum(b_last + m_prev, jnp.max(log_w, axis=0, keepdims=True))
        kws.append(ks[h] * jnp.exp(log_w - m_new))
        decays.append(jnp.exp(b_last + m_prev - m_new))
        mstate[h:h + 1, :] = jnp.broadcast_to(m_new, (1, LANE))

    svs = [_dotb(ss[h], v_augs[h]) for h in heads]
    c_upd = [_dotb_tn(kws[h], v_augs[h]) for h in heads]
    for h in heads:
        cstate[h] = decays[h] * c_prevs[h] + c_upd[h]
        numden = svs[h] + inters[h] * qcs[h]
        num = numden[:, :M_DV]
        den = numden[:, M_DV:M_DV + 1]
        hh = num / jnp.maximum(jnp.abs(den), jnp.exp(-m_ts[h]))
        y = hh * lax.rsqrt(jnp.mean(hh * hh, axis=-1, keepdims=True) + EPS)
        y = y * nw[:, h * M_DV:(h + 1) * M_DV]
        y = y * _sigmoid(o_all[:, h * M_DV:(h + 1) * M_DV])
        y_ref[:, h * M_DV:(h + 1) * M_DV] = y.astype(y_ref.dtype)


def _mlstm(z3, conv_w, gate_bias, norm_w):
    b, t, _ = z3.shape
    L = M_CHUNK

    def zspec(width, col):
        return pl.BlockSpec((None, L, width), lambda i, c: (i, c, col // width))

    def full(shape):
        return pl.BlockSpec(shape, lambda i, c: (0,) * len(shape))

    return pl.pallas_call(
        _mlstm_kernel,
        grid=(b, t // L),
        in_specs=[zspec(1024, Z_MQK), zspec(1024, Z_MV), zspec(1024, Z_MO), zspec(LANE, Z_MIF),
                  full((M_CONV, 1024)), full((1, LANE)), full((1, D_BRANCH))],
        out_specs=pl.BlockSpec((None, L, D_BRANCH), lambda i, c: (i, c, 0)),
        out_shape=jax.ShapeDtypeStruct((b, t, D_BRANCH), BF16),
        scratch_shapes=[pltpu.VMEM((L + 8, 1024), F32),
                        pltpu.VMEM((M_HEADS, M_DK, M_DV + LANE), F32),
                        pltpu.VMEM((8, LANE), F32)],
        compiler_params=_params(("parallel", "arbitrary")),
        name="mlstm",
    )(z3, z3, z3, z3, conv_w, gate_bias, norm_w)


def _gla_intra(q, k, bc, col_masks):
    L = q.shape[0]
    lane = _iota((G_SUB, L), 1)
    blocks = []
    for c in range(L // G_SUB):
        r0 = c * G_SUB
        qb = q[r0:r0 + G_SUB]
        bcb = bc[r0:r0 + G_SUB]
        acc = jnp.zeros((G_SUB, L), F32)
        if c > 0:
            ref = bc[r0 - 1:r0]
            qc = qb * jnp.exp(bcb - ref)
            kc = k * jnp.exp(jnp.minimum(ref - bc, 0.0))
            acc = jnp.where(lane < r0, _dotb_nt(qc, kc), 0.0)
        for j in range(G_SUB):
            s = r0 + j
            e = jnp.exp(jnp.minimum(bcb - bc[s:s + 1], 0.0))
            col = jnp.sum(qb * e * k[s:s + 1], axis=-1, keepdims=True)
            acc = jnp.where(col_masks[c][j], col, acc)
        blocks.append(acc)
    return jnp.concatenate(blocks, axis=0)


def _gla_kernel(qk_ref, v_ref, go_ref, gz_ref, gkup_ref, gkb_ref, normw_ref, y_ref, sstate):
    L = G_CHUNK

    @pl.when(pl.program_id(1) == 0)
    def _():
        sstate[...] = jnp.zeros_like(sstate)

    x = _dot32(gz_ref[...], gkup_ref[...]) + gkb_ref[...]
    log_a = _log_sigmoid(x) * (1.0 / G_TAU)
    bc = _dot32(_tril_ones(L), log_a)
    last = bc[L - 1:L, :]
    e_pos = jnp.exp(bc)
    e_tail = jnp.exp(last - bc)
    e_last = jnp.exp(last)

    sub_lane = _iota((G_SUB, L), 1)
    sub_row = _iota((G_SUB, L), 0)
    col_masks = [[(sub_lane == c * G_SUB + j) & (sub_row >= j) for j in range(G_SUB)]
                 for c in range(L // G_SUB)]
    qk = qk_ref[...]
    v_all = v_ref[...]
    go = go_ref[...]
    nw = normw_ref[...]

    heads = range(G_HEADS)
    sls = [slice(h * G_DK, (h + 1) * G_DK) for h in heads]
    vss = [slice(h * G_DV, (h + 1) * G_DV) for h in heads]
    ks = [qk[:, G_HEADS * G_DK + h * G_DK:G_HEADS * G_DK + (h + 1) * G_DK] for h in heads]
    qss = [qk[:, sls[h]] * (G_DK ** -0.5) for h in heads]
    qds = [qss[h] * e_pos[:, sls[h]] for h in heads]
    s_prevs = [sstate[h] for h in heads]
    atts = [_gla_intra(qss[h], ks[h], bc[:, sls[h]], col_masks) for h in heads]
    o_inter = [_dotb_nt(qds[h], s_prevs[h]) for h in heads]
    o_intra = [_dotb(atts[h], v_all[:, vss[h]]) for h in heads]
    s_upd = [_dotb_tn(v_all[:, vss[h]], ks[h] * e_tail[:, sls[h]]) for h in heads]
    for h in heads:
        sstate[h] = s_prevs[h] * e_last[:, sls[h]] + s_upd[h]
        o = o_intra[h] + o_inter[h]
        y = o * lax.rsqrt(jnp.mean(o * o, axis=-1, keepdims=True) + EPS) * nw[:, vss[h]]
        g = go[:, vss[h]]
        y_ref[:, vss[h]] = (y * (g * _sigmoid(g))).astype(y_ref.dtype)


def _gla(z3, gk_up, gk_bias, norm_w):
    b, t, _ = z3.shape
    L = G_CHUNK

    def zspec(width, col):
        return pl.BlockSpec((None, L, width), lambda i, c: (i, c, col // width))

    def full(shape):
        return pl.BlockSpec(shape, lambda i, c: (0,) * len(shape))

    return pl.pallas_call(
        _gla_kernel,
        grid=(b, t // L),
        in_specs=[zspec(1024, Z_GQK), zspec(1024, Z_GV), zspec(1024, Z_GO), zspec(LANE, Z_GZ),
                  full((LANE, G_HEADS * G_DK)), full((1, G_HEADS * G_DK)), full((1, D_BRANCH))],
        out_specs=pl.BlockSpec((None, L, D_BRANCH), lambda i, c: (i, c, 0)),
        out_shape=jax.ShapeDtypeStruct((b, t, D_BRANCH), BF16),
        scratch_shapes=[pltpu.VMEM((G_HEADS, G_DV, G_DK), F32)],
        compiler_params=_params(("parallel", "arbitrary")),
        name="gla",
    )(z3, z3, z3, z3, gk_up, gk_bias, norm_w)


def _rwkv_kernel(has_vres, *refs):
    if has_vres:
        (r_ref, k_ref, v_ref, lora_ref, vfirst_ref, mu_ref, mul_ref, w0_ref, w2_ref, a0_ref, a2_ref,
         g2_ref, kk_ref, ka_ref, rk_ref, lnw_ref, lnb_ref, v0_ref, v1_ref, v2_ref,
         y_ref, zbuf, sstate) = refs
    else:
        (r_ref, k_ref, v_ref, lora_ref, mu_ref, mul_ref, w0_ref, w2_ref, a0_ref, a2_ref,
         g2_ref, kk_ref, ka_ref, rk_ref, lnw_ref, lnb_ref,
         y_ref, vout_ref, zbuf, sstate) = refs
    L = R_CHUNK
    H2 = 2 * R_HEAD_DIM
    n_pairs = R_HEADS // 2

    @pl.when(pl.program_id(1) == 0)
    def _():
        zbuf[0:8, :] = jnp.zeros((8, R_COLS), F32)
        sstate[...] = jnp.zeros_like(sstate)

    zbuf[8:8 + L, 0:1024] = r_ref[...]
    zbuf[8:8 + L, 1024:2048] = k_ref[...]
    zbuf[8:8 + L, 2048:3072] = v_ref[...]
    zbuf[8:8 + L, 3072:R_COLS] = lora_ref[...]
    cur = zbuf[8:8 + L, :]
    prev = zbuf[7:7 + L, :]
    zbuf[0:8, :] = zbuf[L:L + 8, :]
    mu_all = jnp.concatenate([mu_ref[...], mul_ref[...]], axis=1)
    zs = cur + (prev - cur) * mu_all
    r = zs[:, 0:1024]
    k = zs[:, 1024:2048]
    v = zs[:, 2048:3072]
    zwa = zs[:, 3072:3200]
    zg = zs[:, 3200:3328]

    w_log = _log_sigmoid(w0_ref[...] + _dotb(jnp.tanh(zwa), w2_ref[...])) - 0.5
    lw = -jnp.exp(w_log)
    a_lr = _sigmoid(a0_ref[...] + _dotb(zwa, a2_ref[...]))
    g = _dotb(_sigmoid(zg), g2_ref[...])
    if has_vres:
        mix = _sigmoid(v0_ref[...] + _dotb(_dotb(v, v1_ref[...]), v2_ref[...]))
        v = v + (vfirst_ref[...] - v) * mix
    else:
        vout_ref[...] = v

    cum = _dot32(_tril_ones(L), lw)
    cum_last = cum[L - 1:L, :]
    gam = jnp.exp(cum)
    gam_prev = jnp.exp(cum - lw)
    gam_inv = jnp.exp(-cum)
    gam_tail = jnp.exp(cum_last - cum)
    gam_last = jnp.exp(cum_last)

    kk = k * kk_ref[...]
    k_fin = k * (1.0 + (a_lr - 1.0) * ka_ref[...])
    rkr = r * k_fin * rk_ref[...]

    row = _iota((H2, H2), 0)
    col = _iota((H2, H2), 1)
    same_half = (row < R_HEAD_DIM) == (col < R_HEAD_DIM)
    bd = same_half.astype(F32)
    anti = 1.0 - bd
    t_in = row % R_HEAD_DIM
    s_in = col % R_HEAD_DIM
    strict = (s_in < t_in).astype(F32)
    gram_mask = jnp.where(row < R_HEAD_DIM, strict, (s_in <= t_in).astype(F32))
    lane_lo = (_iota((L, H2), 1) < R_HEAD_DIM)
    eye = (row == col).astype(F32)
    ones_bd = bd.astype(BF16)
    lnw = lnw_ref[...]
    lnb = lnb_ref[...]

    pairs = range(n_pairs)
    sls = [slice(p * H2, (p + 1) * H2) for p in pairs]
    lane_lo2 = jnp.concatenate([lane_lo, lane_lo], axis=0)

    def pair_sums(xs, split):
        his = [x.astype(BF16) for x in xs]
        a = [jnp.dot(h, ones_bd, preferred_element_type=F32) for h in his]
        if not split:
            return a
        los = [(x - h.astype(F32)).astype(BF16) for x, h in zip(xs, his)]
        b = [jnp.dot(l, ones_bd, preferred_element_type=F32) for l in los]
        return [x + y for x, y in zip(a, b)]

    v_ps = [v[:, sl] for sl in sls]
    sums = pair_sums([jnp.concatenate([kk[:, sl] * kk[:, sl], rkr[:, sl]], axis=0) for sl in sls], True)
    x_ars, y_bks, y_kbs, bk_hs = [], [], [], []
    for p, sl in zip(pairs, sls):
        kk_n = kk[:, sl] / jnp.maximum(jnp.sqrt(sums[p][0:L]), 1e-12)
        b_p = kk_n * a_lr[:, sl]
        a_t = -kk_n * gam_prev[:, sl]
        r_t = r[:, sl] * gam[:, sl]
        b_t = b_p * gam_inv[:, sl]
        k_t = k_fin[:, sl] * gam_inv[:, sl]
        x_ars.append(jnp.concatenate([a_t, r_t], axis=0))
        y_bks.append(jnp.concatenate([b_t, k_t], axis=0))
        y_kbs.append(jnp.concatenate([k_t, b_t], axis=0))
        bk_hs.append(jnp.concatenate([b_p * gam_tail[:, sl], k_fin[:, sl] * gam_tail[:, sl]], axis=0))

    g1s = [_dotb_nt(jnp.where(lane_lo2, x_ars[p], 0.0), y_bks[p]) * gram_mask for p in pairs]
    g2s = [_dotb_nt(jnp.where(lane_lo2, 0.0, x_ars[p]), y_kbs[p]) * gram_mask for p in pairs]

    pws = [jnp.concatenate([g1s[p][0:L], g2s[p][0:L]], axis=0) * bd for p in pairs]
    t_invs = [eye + pws[p] for p in pairs]
    for _ in range(5):
        pws = [_dotb(pw, pw) for pw in pws]
        t_upd = [_dotb(pws[p], t_invs[p]) for p in pairs]
        t_invs = [t_invs[p] + t_upd[p] for p in pairs]

    m2s = [_dotb(jnp.where(lane_lo2, g2s[p], g1s[p]),
                 jnp.concatenate([v_ps[p], v_ps[p]], axis=0) * anti) for p in pairs]
    s_prevs = [sstate[p] for p in pairs]
    m1s = [_dotb_nt(x_ars[p], s_prevs[p]) for p in pairs]
    rhss = [m1s[p][0:L] + m2s[p][0:L] for p in pairs]
    uxs = [_dotb(t_invs[p], jnp.concatenate([rhss[p], rhss[p]], axis=0) * bd) for p in pairs]
    us = [ux[0:L] + ux[L:] for ux in uxs]
    rbus = [_dotb(jnp.where(lane_lo, g1s[p][L:], g2s[p][L:]),
                  jnp.concatenate([us[p], us[p]], axis=0) * bd) for p in pairs]
    s_upd = [_dotb_tn(jnp.concatenate([us[p], v_ps[p]], axis=0), bk_hs[p]) for p in pairs]
    for p, sl in zip(pairs, sls):
        sstate[p] = s_prevs[p] * gam_last[:, sl] + s_upd[p] * bd

    ys = [m1s[p][L:] + m2s[p][L:] + rbus[p] for p in pairs]
    means = pair_sums(ys, False)
    ycs = [ys[p] - means[p] * (1.0 / R_HEAD_DIM) for p in pairs]
    vars_ = pair_sums([yc * yc for yc in ycs], False)
    for p, sl in zip(pairs, sls):
        yn = ycs[p] * lax.rsqrt(vars_[p] * (1.0 / R_HEAD_DIM) + R_GN_EPS) * lnw[:, sl] + lnb[:, sl]
        yn = yn + sums[p][L:] * v_ps[p]
        y_ref[:, sl] = (yn * g[:, sl]).astype(y_ref.dtype)


def _rwkv(z3, v_first, p):
    b, t, _ = z3.shape
    L = R_CHUNK
    has_vres = v_first is not None

    def zspec(width, col):
        return pl.BlockSpec((None, L, width), lambda i, c: (i, c, col // width))

    def full(a):
        return pl.BlockSpec(a.shape, lambda i, c: (0,) * a.ndim)

    seq_spec = pl.BlockSpec((None, L, D_BRANCH), lambda i, c: (i, c, 0))
    names = ["mu", "mul", "w0", "w2", "a0", "a2", "g2", "kk", "ka", "rk", "lnw", "lnb"]
    if has_vres:
        names += ["v0", "v1", "v2"]
    consts = [p[n] for n in names]
    z_specs = [zspec(1024, Z_R), zspec(1024, Z_K), zspec(1024, Z_V), zspec(256, Z_LORA)]
    args = [z3, z3, z3, z3]
    if has_vres:
        z_specs.append(seq_spec)
        args.append(v_first)
    y_shape = jax.ShapeDtypeStruct((b, t, D_BRANCH), BF16)
    if has_vres:
        out_shape, out_specs = y_shape, seq_spec
    else:
        out_shape = (y_shape, jax.ShapeDtypeStruct((b, t, D_BRANCH), F32))
        out_specs = (seq_spec, seq_spec)
    return pl.pallas_call(
        functools.partial(_rwkv_kernel, has_vres),
        grid=(b, t // L),
        in_specs=z_specs + [full(a) for a in consts],
        out_specs=out_specs,
        out_shape=out_shape,
        scratch_shapes=[pltpu.VMEM((L + 8, R_COLS), F32),
                        pltpu.VMEM((R_HEADS // 2, 2 * R_HEAD_DIM, 2 * R_HEAD_DIM), F32)],
        compiler_params=_params(("parallel", "arbitrary")),
        name="rwkv7",
    )(*args, *consts)


def _pad_cols(a, n):
    return jnp.pad(a, ((0, 0), (0, n - a.shape[1])))


def _pad_rows(a, n, before=0):
    return jnp.pad(a, ((before, n - a.shape[0] - before), (0, 0)))


def _z_tile_table():
    off = dict(zip(("mq", "mk", "mv", "mo", "mi", "mf", "rz", "gq", "gk", "gv", "gz", "go", "gates"),
                   (0,) + IN_SPLITS))
    r0 = off["rz"]
    segs = [(off["mv"], 1024, 1024), (off["mo"], 1024, 1024),
            (r0, 1024, 1024), (r0 + 1024, 1024, 1024), (r0 + 2048, 1024, 1024),
            (off["gv"], 1024, 1024), (off["go"], 1024, 1024),
            (off["mq"], 1024, 1024), (off["gq"], 1024, 1024),
            (r0 + 3072, 256, 256), (off["mi"], 2 * M_HEADS, LANE), (off["gz"], G_RANK, LANE),
            (off["gates"], N_BRANCH * D_MODEL, N_BRANCH * D_MODEL)]
    start, nvalid = [], []
    for src, valid, width in segs:
        for t in range(width // LANE):
            assert (src + LANE * t) % 8 == 0
            start.append(src + LANE * t)
            nvalid.append(max(0, min(LANE, valid - LANE * t)))
    assert len(start) * LANE == Z_COLS
    return start, nvalid


def _prep_w_in_kernel(start_ref, nvalid_ref, src_ref, o_ref):
    del start_ref
    r = _iota((LANE, LANE), 0)
    c = _iota((LANE, LANE), 1)
    select = ((r == c) & (c < nvalid_ref[pl.program_id(0)])).astype(BF16)
    o_ref[...] = _dotb_tn(src_ref[...], select).astype(o_ref.dtype)


def _prep_w_in(w_all_t, layer):
    n_layers, n_src, d = w_all_t.shape
    start, nvalid = (jnp.asarray(a, jnp.int32) for a in _z_tile_table())
    grid_spec = pltpu.PrefetchScalarGridSpec(
        num_scalar_prefetch=2,
        grid=(Z_COLS // LANE,),
        in_specs=[pl.BlockSpec((pl.Element(LANE), pl.Element(d)), lambda j, s, n: (s[j] * 8, 0))],
        out_specs=pl.BlockSpec((d, LANE), lambda j, s, n: (0, j)),
    )
    return pl.pallas_call(
        _prep_w_in_kernel,
        grid_spec=grid_spec,
        out_shape=jax.ShapeDtypeStruct((d, Z_COLS), BF16),
        compiler_params=_params(("parallel",)),
        name="prep_w_in",
    )((start + layer * n_src) // 8, nvalid, w_all_t.reshape(n_layers * n_src, d))


def _row(a):
    return a.reshape(1, -1).astype(F32)


def _pick_tm(m, cap):
    tm = min(m, cap)
    while m % tm:
        tm //= 2
    return tm


def kernel(x, ffn1_norm, ffn1_w_in, ffn1_w_out, mix_norm, w_in, m_conv, m_i_bias, m_f_bias, m_norm, r_mu, r_w0, r_w2, r_a0, r_a2, r_g2, r_k_k, r_k_a, r_r_k, r_ln_w, r_ln_b, r_v0, r_v1, r_v2, g_gk_up, g_gk_bias, g_norm, w_branch, w_out, ffn2_norm, ffn2_w_in, ffn2_w_out, final_norm):
    bsz, seq, d = x.shape
    m = bsz * seq
    depth = w_in.shape[0]
    xs = x.reshape(m, d)
    tm_big = _pick_tm(m, 2048)
    tm_mid = _pick_tm(m, 1024)
    tm_small = _pick_tm(m, 512)

    def ffn(xs, norm_w, w1_all, w2_all, layer):
        h = _rmsnorm(xs, norm_w, BF16)
        act = _ffn_in(h, w1_all, layer, tm_big)
        return _matmul_res(act, w2_all, layer, xs, 0.5, tm_small, 512, "ffn_out")

    ffn1_w_out_b = ffn1_w_out.astype(BF16)
    ffn2_w_out_b = ffn2_w_out.astype(BF16)
    w_branch_b = w_branch.astype(BF16)
    w_out_b = w_out.astype(BF16)
    w_in_t = jnp.swapaxes(w_in, 1, 2)

    v_first = None
    for l in range(depth):
        xs = ffn(xs, ffn1_norm[l], ffn1_w_in, ffn1_w_out_b, l)

        h = _rmsnorm(xs, mix_norm[l], BF16)
        z2 = _matmul(h, _prep_w_in(w_in_t, l), tm_big, 512, F32)
        z3 = z2.reshape(bsz, seq, Z_COLS)

        gate_bias = _pad_cols(jnp.concatenate([m_i_bias[l], m_f_bias[l]]).reshape(1, -1), LANE)
        y_m = _mlstm(z3, m_conv[l], gate_bias, _row(m_norm[l]))

        mu = r_mu[l]
        rp = {
            "mu": _row(mu[:3 * D_BRANCH]), "mul": _row(mu[3 * D_BRANCH:]),
            "w0": _row(r_w0[l]), "w2": _pad_rows(r_w2[l], LANE).astype(BF16),
            "a0": _row(r_a0[l]), "a2": _pad_rows(r_a2[l], LANE, before=R_W_RANK).astype(BF16),
            "g2": r_g2[l].astype(BF16),
            "kk": _row(r_k_k[l]), "ka": _row(r_k_a[l]), "rk": _row(r_r_k[l]),
            "lnw": _row(r_ln_w[l]), "lnb": _row(r_ln_b[l]),
        }
        if l == 0:
            y_r, v_first = _rwkv(z3, None, rp)
        else:
            rp["v0"] = _row(r_v0[l - 1])
            rp["v1"] = _pad_cols(r_v1[l - 1], LANE).astype(BF16)
            rp["v2"] = _pad_rows(r_v2[l - 1], LANE).astype(BF16)
            y_r = _rwkv(z3, v_first, rp)

        y_g = _gla(z3, _pad_rows(g_gk_up[l], LANE), _row(g_gk_bias[l]), _row(g_norm[l]))

        merged = _merge(y_m.reshape(m, D_BRANCH), y_r.reshape(m, D_BRANCH), y_g.reshape(m, D_BRANCH),
                        w_branch_b, l, z2, tm_mid)
        xs = _matmul_res(merged, w_out_b, l, xs, 1.0, tm_mid, 512, "mix_out")

        xs = ffn(xs, ffn2_norm[l], ffn2_w_in, ffn2_w_out_b, l)

    out = _rmsnorm(xs, final_norm, F32)
    return out.reshape(bsz, seq, d)
```

```python
import functools

import jax
import jax.numpy as jnp
from jax import lax
from jax.experimental import pallas as pl
from jax.experimental.pallas import tpu as pltpu

F32 = jnp.float32
BF16 = jnp.bfloat16

D_MODEL = 4096
D_BRANCH = 1024
N_BRANCH = 3
D_FF = 11008
EPS = 1e-6
M_HEADS, M_DV, M_DK, M_CONV, M_GATE_CAP = 4, 256, 128, 4, 15.0
R_HEAD_DIM, R_HEADS = 64, 16
R_W_RANK, R_A_RANK, R_G_RANK, R_V_RANK = 64, 64, 128, 32
R_GN_EPS = 64e-5
G_HEADS, G_DV, G_DK, G_RANK, G_TAU = 4, 256, 128, 16, 16.0

M_SIZES = (M_HEADS * M_DK, M_HEADS * M_DK, D_BRANCH, D_BRANCH, M_HEADS, M_HEADS)
R_SIZES = (D_BRANCH, D_BRANCH, D_BRANCH, R_W_RANK, R_A_RANK, R_G_RANK)
G_SIZES = (G_HEADS * G_DK, G_HEADS * G_DK, D_BRANCH, G_RANK, D_BRANCH)
R_COLS = sum(R_SIZES)
IN_SIZES = M_SIZES + (R_COLS,) + G_SIZES + (N_BRANCH * D_MODEL,)


def _cumsplits(sizes):
    out, acc = [], 0
    for s in sizes[:-1]:
        acc += s
        out.append(acc)
    return tuple(out)


IN_SPLITS = _cumsplits(IN_SIZES)
R_SPLITS = _cumsplits(R_SIZES)

LANE = 128
VMEM_LIMIT = 56 * 1024 * 1024

Z_MV, Z_MO, Z_R, Z_K, Z_V, Z_GV, Z_GO, Z_MQK, Z_GQK = (i * 1024 for i in range(9))
Z_LORA = 9216
Z_MIF = 9472
Z_GZ = 9600
Z_GATES = 9728
Z_COLS = Z_GATES + N_BRANCH * D_MODEL

FF_TN = 256
M_CHUNK = 128
G_CHUNK = 64
G_SUB = 16
R_CHUNK = 64


def _params(sem, vmem=None):
    return pltpu.CompilerParams(dimension_semantics=sem, vmem_limit_bytes=vmem)


def _dotb(a, b):
    return jnp.dot(a.astype(BF16), b.astype(BF16), preferred_element_type=F32)


def _dotb_nt(a, b):
    return lax.dot_general(a.astype(BF16), b.astype(BF16), (((1,), (1,)), ((), ())),
                           preferred_element_type=F32)


def _dotb_tn(a, b):
    return lax.dot_general(a.astype(BF16), b.astype(BF16), (((0,), (0,)), ((), ())),
                           preferred_element_type=F32)


def _dot32(a, b):
    return jnp.dot(a, b, precision=lax.Precision.HIGHEST, preferred_element_type=F32)


def _log_sigmoid(x):
    return jnp.minimum(x, 0.0) - jnp.log1p(jnp.exp(-jnp.abs(x)))


def _sigmoid(x):
    return 0.5 * jnp.tanh(0.5 * x) + 0.5


def _iota(shape, dim):
    return lax.broadcasted_iota(jnp.int32, shape, dim)


def _tril_ones(n):
    return (_iota((n, n), 0) >= _iota((n, n), 1)).astype(F32)


def _rmsnorm_kernel(x_ref, w_ref, o_ref):
    x = x_ref[...]
    ms = jnp.mean(x * x, axis=-1, keepdims=True)
    o_ref[...] = (x * lax.rsqrt(ms + EPS) * w_ref[...]).astype(o_ref.dtype)


def _rmsnorm(x, w, out_dtype, tm=256):
    m, d = x.shape
    return pl.pallas_call(
        _rmsnorm_kernel,
        grid=(m // tm,),
        in_specs=[pl.BlockSpec((tm, d), lambda i: (i, 0)),
                  pl.BlockSpec((1, d), lambda i: (0, 0))],
        out_specs=pl.BlockSpec((tm, d), lambda i: (i, 0)),
        out_shape=jax.ShapeDtypeStruct((m, d), out_dtype),
        compiler_params=_params(("parallel",)),
        name="rmsnorm",
    )(x, w.reshape(1, d))


def _ffn_in_kernel(h_ref, wg_ref, wu_ref, o_ref, wbuf):
    @pl.when(pl.program_id(1) == 0)
    def _():
        wbuf[:, :FF_TN] = wg_ref[...].astype(BF16)
        wbuf[:, FF_TN:] = wu_ref[...].astype(BF16)

    z = jnp.dot(h_ref[...], wbuf[...], preferred_element_type=F32)
    g = z[:, :FF_TN]
    u = z[:, FF_TN:]
    o_ref[...] = (g * _sigmoid(g) * u).astype(o_ref.dtype)


def _ffn_in(h, w_all, layer, tm):
    m, d = h.shape
    n = w_all.shape[2] // 2
    nt = n // FF_TN
    return pl.pallas_call(
        _ffn_in_kernel,
        grid=(nt, m // tm),
        in_specs=[pl.BlockSpec((tm, d), lambda j, i: (i, 0)),
                  pl.BlockSpec((None, d, FF_TN), lambda j, i: (layer, 0, j)),
                  pl.BlockSpec((None, d, FF_TN), lambda j, i: (layer, 0, j + nt))],
        out_specs=pl.BlockSpec((tm, FF_TN), lambda j, i: (i, j)),
        out_shape=jax.ShapeDtypeStruct((m, n), BF16),
        scratch_shapes=[pltpu.VMEM((d, 2 * FF_TN), BF16)],
        compiler_params=_params(("parallel", "arbitrary"), VMEM_LIMIT),
        name="ffn_in",
    )(h, w_all, w_all)


def _matmul_kernel(a_ref, w_ref, o_ref):
    o_ref[...] = jnp.dot(a_ref[...], w_ref[...], preferred_element_type=F32).astype(o_ref.dtype)


def _matmul(a, w, tm, tn, out_dtype):
    m, k = a.shape
    n = w.shape[1]
    return pl.pallas_call(
        _matmul_kernel,
        grid=(m // tm, n // tn),
        in_specs=[pl.BlockSpec((tm, k), lambda i, j: (i, 0)),
                  pl.BlockSpec((k, tn), lambda i, j: (0, j))],
        out_specs=pl.BlockSpec((tm, tn), lambda i, j: (i, j)),
        out_shape=jax.ShapeDtypeStruct((m, n), out_dtype),
        compiler_params=_params(("parallel", "arbitrary"), VMEM_LIMIT),
        name="mix_in",
    )(a, w)


def _matmul_res_kernel(scale, a_ref, w_ref, x_ref, o_ref):
    y = jnp.dot(a_ref[...], w_ref[...], preferred_element_type=F32)
    o_ref[...] = x_ref[...] + scale * y


def _matmul_res(a, w_all, layer, x, scale, tm, tn, name):
    m, k = a.shape
    n = w_all.shape[2]
    return pl.pallas_call(
        functools.partial(_matmul_res_kernel, scale),
        grid=(m // tm, n // tn),
        in_specs=[pl.BlockSpec((tm, k), lambda i, j: (i, 0)),
                  pl.BlockSpec((None, k, tn), lambda i, j: (layer, 0, j)),
                  pl.BlockSpec((tm, tn), lambda i, j: (i, j))],
        out_specs=pl.BlockSpec((tm, tn), lambda i, j: (i, j)),
        out_shape=jax.ShapeDtypeStruct((m, n), F32),
        compiler_params=_params(("parallel", "arbitrary"), VMEM_LIMIT),
        name=name,
    )(a, w_all, x)


def _merge_kernel(ym_ref, yr_ref, yg_ref, wb_ref, g0_ref, g1_ref, g2_ref, o_ref):
    acc = _sigmoid(g0_ref[...]) * jnp.dot(ym_ref[...], wb_ref[0], preferred_element_type=F32)
    acc += _sigmoid(g1_ref[...]) * jnp.dot(yr_ref[...], wb_ref[1], preferred_element_type=F32)
    acc += _sigmoid(g2_ref[...]) * jnp.dot(yg_ref[...], wb_ref[2], preferred_element_type=F32)
    o_ref[...] = acc.astype(o_ref.dtype)


def _merge(ym, yr, yg, wb_all, layer, z2d, tm, tn=512):
    m = ym.shape[0]
    gate_blk = Z_GATES // tn
    per = D_MODEL // tn
    y_spec = pl.BlockSpec((tm, D_BRANCH), lambda i, j: (i, 0))

    def gate_spec(b):
        return pl.BlockSpec((tm, tn), lambda i, j: (i, gate_blk + b * per + j))

    return pl.pallas_call(
        _merge_kernel,
        grid=(m // tm, D_MODEL // tn),
        in_specs=[y_spec, y_spec, y_spec,
                  pl.BlockSpec((None, N_BRANCH, D_BRANCH, tn), lambda i, j: (layer, 0, 0, j)),
                  gate_spec(0), gate_spec(1), gate_spec(2)],
        out_specs=pl.BlockSpec((tm, tn), lambda i, j: (i, j)),
        out_shape=jax.ShapeDtypeStruct((m, D_MODEL), BF16),
        compiler_params=_params(("parallel", "arbitrary"), VMEM_LIMIT),
        name="merge",
    )(ym, yr, yg, wb_all, z2d, z2d, z2d)


def _mlstm_kernel(qk_ref, v_ref, o_ref, mif_ref, convw_ref, bias_ref, normw_ref, y_ref,
                  qkbuf, cstate, mstate):
    L = M_CHUNK

    @pl.when(pl.program_id(1) == 0)
    def _():
        qkbuf[0:8, :] = jnp.zeros((8, 2 * M_HEADS * M_DK), F32)
        cstate[...] = jnp.zeros_like(cstate)
        mstate[...] = jnp.zeros_like(mstate)

    qkbuf[8:8 + L, :] = qk_ref[...]
    cw = convw_ref[...]
    conv = (cw[0:1] * qkbuf[5:5 + L, :] + cw[1:2] * qkbuf[6:6 + L, :]
            + cw[2:3] * qkbuf[7:7 + L, :] + cw[3:4] * qkbuf[8:8 + L, :])
    qkbuf[0:8, :] = qkbuf[L:L + 8, :]
    qk = conv * _sigmoid(conv)

    capped = M_GATE_CAP * jnp.tanh((mif_ref[...] + bias_ref[...]) * (1.0 / M_GATE_CAP))
    logf = _log_sigmoid(capped)
    bcum = _dot32(_tril_ones(L), logf)
    bcum_t = bcum.T
    capped_t = capped.T

    causal = _iota((L, L), 0) >= _iota((L, L), 1)
    ones_col = (_iota((L, LANE), 1) == 0).astype(F32)
    v_all = v_ref[...]
    o_all = o_ref[...]
    nw = normw_ref[...]

    heads = range(M_HEADS)
    qs = [qk[:, h * M_DK:(h + 1) * M_DK] for h in heads]
    ks = [qk[:, (M_HEADS + h) * M_DK:(M_HEADS + h + 1) * M_DK] * (M_DK ** -0.5) for h in heads]
    v_augs = [jnp.concatenate([v_all[:, h * M_DV:(h + 1) * M_DV], ones_col], axis=1) for h in heads]
    c_prevs = [cstate[h] for h in heads]
    qks = [_dotb_nt(qs[h], ks[h]) for h in heads]
    qcs = [_dotb(qs[h], c_prevs[h]) for h in heads]

    ss, inters, m_ts, kws, decays = [], [], [], [], []
    for h in heads:
        b_col = bcum[:, M_HEADS + h:M_HEADS + h + 1]
        i_col = capped[:, h:h + 1]
        b_row = bcum_t[M_HEADS + h:M_HEADS + h + 1, :]
        i_row = capped_t[h:h + 1, :]
        m_prev = mstate[h:h + 1, 0:1]
        log_d = jnp.where(causal, b_col - b_row + i_row, -jnp.inf)
        log_inter = b_col + m_prev
        m_t = jnp.maximum(log_inter, jnp.max(log_d, axis=-1, keepdims=True))
        ss.append(qks[h] * jnp.exp(log_d - m_t))
        inters.append(jnp.exp(log_inter - m_t))
        m_ts.append(m_t)

        b_last = b_col[L - 1:L, :]
        log_w = b_last - b_col + i_col
        m_new = jnp.maximum(b_last + m_prev, jnp.max(log_w, axis=0, keepdims=True))
        kws.append(ks[h] * jnp.exp(log_w - m_new))
        decays.append(jnp.exp(b_last + m_prev - m_new))
        mstate[h:h + 1, :] = jnp.broadcast_to(m_new, (1, LANE))

    svs = [_dotb(ss[h], v_augs[h]) for h in heads]
    c_upd = [_dotb_tn(kws[h], v_augs[h]) for h in heads]
    for h in heads:
        cstate[h] = decays[h] * c_prevs[h] + c_upd[h]
        numden = svs[h] + inters[h] * qcs[h]
        num = numden[:, :M_DV]
        den = numden[:, M_DV:M_DV + 1]
        hh = num / jnp.maximum(jnp.abs(den), jnp.exp(-m_ts[h]))
        y = hh * lax.rsqrt(jnp.mean(hh * hh, axis=-1, keepdims=True) + EPS)
        y = y * nw[:, h * M_DV:(h + 1) * M_DV]
        y = y * _sigmoid(o_all[:, h * M_DV:(h + 1) * M_DV])
        y_ref[:, h * M_DV:(h + 1) * M_DV] = y.astype(y_ref.dtype)


def _mlstm(z3, conv_w, gate_bias, norm_w):
    b, t, _ = z3.shape
    L = M_CHUNK

    def zspec(width, col):
        return pl.BlockSpec((None, L, width), lambda i, c: (i, c, col // width))

    def full(shape):
        return pl.BlockSpec(shape, lambda i, c: (0,) * len(shape))

    return pl.pallas_call(
        _mlstm_kernel,
        grid=(b, t // L),
        in_specs=[zspec(1024, Z_MQK), zspec(1024, Z_MV), zspec(1024, Z_MO), zspec(LANE, Z_MIF),
                  full((M_CONV, 1024)), full((1, LANE)), full((1, D_BRANCH))],
        out_specs=pl.BlockSpec((None, L, D_BRANCH), lambda i, c: (i, c, 0)),
        out_shape=jax.ShapeDtypeStruct((b, t, D_BRANCH), BF16),
        scratch_shapes=[pltpu.VMEM((L + 8, 1024), F32),
                        pltpu.VMEM((M_HEADS, M_DK, M_DV + LANE), F32),
                        pltpu.VMEM((8, LANE), F32)],
        compiler_params=_params(("parallel", "arbitrary")),
        name="mlstm",
    )(z3, z3, z3, z3, conv_w, gate_bias, norm_w)


def _gla_intra(q, k, bc, col_masks):
    L = q.shape[0]
    lane = _iota((G_SUB, L), 1)
    blocks = []
    for c in range(L // G_SUB):
        r0 = c * G_SUB
        qb = q[r0:r0 + G_SUB]
        bcb = bc[r0:r0 + G_SUB]
        acc = jnp.zeros((G_SUB, L), F32)
        if c > 0:
            ref = bc[r0 - 1:r0]
            qc = qb * jnp.exp(bcb - ref)
            kc = k * jnp.exp(jnp.minimum(ref - bc, 0.0))
            acc = jnp.where(lane < r0, _dotb_nt(qc, kc), 0.0)
        for j in range(G_SUB):
            s = r0 + j
            e = jnp.exp(jnp.minimum(bcb - bc[s:s + 1], 0.0))
            col = jnp.sum(qb * e * k[s:s + 1], axis=-1, keepdims=True)
            acc = jnp.where(col_masks[c][j], col, acc)
        blocks.append(acc)
    return jnp.concatenate(blocks, axis=0)


def _gla_kernel(qk_ref, v_ref, go_ref, gz_ref, gkup_ref, gkb_ref, normw_ref, y_ref, sstate):
    L = G_CHUNK

    @pl.when(pl.program_id(1) == 0)
    def _():
        sstate[...] = jnp.zeros_like(sstate)

    x = _dot32(gz_ref[...], gkup_ref[...]) + gkb_ref[...]
    log_a = _log_sigmoid(x) * (1.0 / G_TAU)
    bc = _dot32(_tril_ones(L), log_a)
    last = bc[L - 1:L, :]
    e_pos = jnp.exp(bc)
    e_tail = jnp.exp(last - bc)
    e_last = jnp.exp(last)

    sub_lane = _iota((G_SUB, L), 1)
    sub_row = _iota((G_SUB, L), 0)
    col_masks = [[(sub_lane == c * G_SUB + j) & (sub_row >= j) for j in range(G_SUB)]
                 for c in range(L // G_SUB)]
    qk = qk_ref[...]
    v_all = v_ref[...]
    go = go_ref[...]
    nw = normw_ref[...]

    heads = range(G_HEADS)
    sls = [slice(h * G_DK, (h + 1) * G_DK) for h in heads]
    vss = [slice(h * G_DV, (h + 1) * G_DV) for h in heads]
    ks = [qk[:, G_HEADS * G_DK + h * G_DK:G_HEADS * G_DK + (h + 1) * G_DK] for h in heads]
    qss = [qk[:, sls[h]] * (G_DK ** -0.5) for h in heads]
    qds = [qss[h] * e_pos[:, sls[h]] for h in heads]
    s_prevs = [sstate[h] for h in heads]
    atts = [_gla_intra(qss[h], ks[h], bc[:, sls[h]], col_masks) for h in heads]
    o_inter = [_dotb_nt(qds[h], s_prevs[h]) for h in heads]
    o_intra = [_dotb(atts[h], v_all[:, vss[h]]) for h in heads]
    s_upd = [_dotb_tn(v_all[:, vss[h]], ks[h] * e_tail[:, sls[h]]) for h in heads]
    for h in heads:
        sstate[h] = s_prevs[h] * e_last[:, sls[h]] + s_upd[h]
        o = o_intra[h] + o_inter[h]
        y = o * lax.rsqrt(jnp.mean(o * o, axis=-1, keepdims=True) + EPS) * nw[:, vss[h]]
        g = go[:, vss[h]]
        y_ref[:, vss[h]] = (y * (g * _sigmoid(g))).astype(y_ref.dtype)


def _gla(z3, gk_up, gk_bias, norm_w):
    b, t, _ = z3.shape
    L = G_CHUNK

    def zspec(width, col):
        return pl.BlockSpec((None, L, width), lambda i, c: (i, c, col // width))

    def full(shape):
        return pl.BlockSpec(shape, lambda i, c: (0,) * len(shape))

    return pl.pallas_call(
        _gla_kernel,
        grid=(b, t // L),
        in_specs=[zspec(1024, Z_GQK), zspec(1024, Z_GV), zspec(1024, Z_GO), zspec(LANE, Z_GZ),
                  full((LANE, G_HEADS * G_DK)), full((1, G_HEADS * G_DK)), full((1, D_BRANCH))],
        out_specs=pl.BlockSpec((None, L, D_BRANCH), lambda i, c: (i, c, 0)),
        out_shape=jax.ShapeDtypeStruct((b, t, D_BRANCH), BF16),
        scratch_shapes=[pltpu.VMEM((G_HEADS, G_DV, G_DK), F32)],
        compiler_params=_params(("parallel", "arbitrary")),
        name="gla",
    )(z3, z3, z3, z3, gk_up, gk_bias, norm_w)


def _rwkv_kernel(has_vres, *refs):
    if has_vres:
        (r_ref, k_ref, v_ref, lora_ref, vfirst_ref, mu_ref, mul_ref, w0_ref, w2_ref, a0_ref, a2_ref,
         g2_ref, kk_ref, ka_ref, rk_ref, lnw_ref, lnb_ref, v0_ref, v1_ref, v2_ref,
         y_ref, zbuf, sstate) = refs
    else:
        (r_ref, k_ref, v_ref, lora_ref, mu_ref, mul_ref, w0_ref, w2_ref, a0_ref, a2_ref,
         g2_ref, kk_ref, ka_ref, rk_ref, lnw_ref, lnb_ref,
         y_ref, vout_ref, zbuf, sstate) = refs
    L = R_CHUNK
    H2 = 2 * R_HEAD_DIM
    n_pairs = R_HEADS // 2

    @pl.when(pl.program_id(1) == 0)
    def _():
        zbuf[0:8, :] = jnp.zeros((8, R_COLS), F32)
        sstate[...] = jnp.zeros_like(sstate)

    zbuf[8:8 + L, 0:1024] = r_ref[...]
    zbuf[8:8 + L, 1024:2048] = k_ref[...]
    zbuf[8:8 + L, 2048:3072] = v_ref[...]
    zbuf[8:8 + L, 3072:R_COLS] = lora_ref[...]
    cur = zbuf[8:8 + L, :]
    prev = zbuf[7:7 + L, :]
    zbuf[0:8, :] = zbuf[L:L + 8, :]
    mu_all = jnp.concatenate([mu_ref[...], mul_ref[...]], axis=1)
    zs = cur + (prev - cur) * mu_all
    r = zs[:, 0:1024]
    k = zs[:, 1024:2048]
    v = zs[:, 2048:3072]
    zwa = zs[:, 3072:3200]
    zg = zs[:, 3200:3328]

    w_log = _log_sigmoid(w0_ref[...] + _dotb(jnp.tanh(zwa), w2_ref[...])) - 0.5
    lw = -jnp.exp(w_log)
    a_lr = _sigmoid(a0_ref[...] + _dotb(zwa, a2_ref[...]))
    g = _dotb(_sigmoid(zg), g2_ref[...])
    if has_vres:
        mix = _sigmoid(v0_ref[...] + _dotb(_dotb(v, v1_ref[...]), v2_ref[...]))
        v = v + (vfirst_ref[...] - v) * mix
    else:
        vout_ref[...] = v

    cum = _dot32(_tril_ones(L), lw)
    cum_last = cum[L - 1:L, :]
    gam = jnp.exp(cum)
    gam_prev = jnp.exp(cum - lw)
    gam_inv = jnp.exp(-cum)
    gam_tail = jnp.exp(cum_last - cum)
    gam_last = jnp.exp(cum_last)

    kk = k * kk_ref[...]
    k_fin = k * (1.0 + (a_lr - 1.0) * ka_ref[...])
    rkr = r * k_fin * rk_ref[...]

    row = _iota((H2, H2), 0)
    col = _iota((H2, H2), 1)
    same_half = (row < R_HEAD_DIM) == (col < R_HEAD_DIM)
    bd = same_half.astype(F32)
    anti = 1.0 - bd
    t_in = row % R_HEAD_DIM
    s_in = col % R_HEAD_DIM
    strict = (s_in < t_in).astype(F32)
    gram_mask = jnp.where(row < R_HEAD_DIM, strict, (s_in <= t_in).astype(F32))
    lane_lo = (_iota((L, H2), 1) < R_HEAD_DIM)
    eye = (row == col).astype(F32)
    ones_bd = bd.astype(BF16)
    lnw = lnw_ref[...]
    lnb = lnb_ref[...]

    pairs = range(n_pairs)
    sls = [slice(p * H2, (p + 1) * H2) for p in pairs]
    lane_lo2 = jnp.concatenate([lane_lo, lane_lo], axis=0)

    def pair_sums(xs, split):
        his = [x.astype(BF16) for x in xs]
        a = [jnp.dot(h, ones_bd, preferred_element_type=F32) for h in his]
        if not split:
            return a
        los = [(x - h.astype(F32)).astype(BF16) for x, h in zip(xs, his)]
        b = [jnp.dot(l, ones_bd, preferred_element_type=F32) for l in los]
        return [x + y for x, y in zip(a, b)]

    v_ps = [v[:, sl] for sl in sls]
    sums = pair_sums([jnp.concatenate([kk[:, sl] * kk[:, sl], rkr[:, sl]], axis=0) for sl in sls], True)
    x_ars, y_bks, y_kbs, bk_hs = [], [], [], []
    for p, sl in zip(pairs, sls):
        kk_n = kk[:, sl] / jnp.maximum(jnp.sqrt(sums[p][0:L]), 1e-12)
        b_p = kk_n * a_lr[:, sl]
        a_t = -kk_n * gam_prev[:, sl]
        r_t = r[:, sl] * gam[:, sl]
        b_t = b_p * gam_inv[:, sl]
        k_t = k_fin[:, sl] * gam_inv[:, sl]
        x_ars.append(jnp.concatenate([a_t, r_t], axis=0))
        y_bks.append(jnp.concatenate([b_t, k_t], axis=0))
        y_kbs.append(jnp.concatenate([k_t, b_t], axis=0))
        bk_hs.append(jnp.concatenate([b_p * gam_tail[:, sl], k_fin[:, sl] * gam_tail[:, sl]], axis=0))

    g1s = [_dotb_nt(jnp.where(lane_lo2, x_ars[p], 0.0), y_bks[p]) * gram_mask for p in pairs]
    g2s = [_dotb_nt(jnp.where(lane_lo2, 0.0, x_ars[p]), y_kbs[p]) * gram_mask for p in pairs]

    pws = [jnp.concatenate([g1s[p][0:L], g2s[p][0:L]], axis=0) * bd for p in pairs]
    t_invs = [eye + pws[p] for p in pairs]
    for _ in range(5):
        pws = [_dotb(pw, pw) for pw in pws]
        t_upd = [_dotb(pws[p], t_invs[p]) for p in pairs]
        t_invs = [t_invs[p] + t_upd[p] for p in pairs]

    m2s = [_dotb(jnp.where(lane_lo2, g2s[p], g1s[p]),
                 jnp.concatenate([v_ps[p], v_ps[p]], axis=0) * anti) for p in pairs]
    s_prevs = [sstate[p] for p in pairs]
    m1s = [_dotb_nt(x_ars[p], s_prevs[p]) for p in pairs]
    rhss = [m1s[p][0:L] + m2s[p][0:L] for p in pairs]
    uxs = [_dotb(t_invs[p], jnp.concatenate([rhss[p], rhss[p]], axis=0) * bd) for p in pairs]
    us = [ux[0:L] + ux[L:] for ux in uxs]
    rbus = [_dotb(jnp.where(lane_lo, g1s[p][L:], g2s[p][L:]),
                  jnp.concatenate([us[p], us[p]], axis=0) * bd) for p in pairs]
    s_upd = [_dotb_tn(jnp.concatenate([us[p], v_ps[p]], axis=0), bk_hs[p]) for p in pairs]
    for p, sl in zip(pairs, sls):
        sstate[p] = s_prevs[p] * gam_last[:, sl] + s_upd[p] * bd

    ys = [m1s[p][L:] + m2s[p][L:] + rbus[p] for p in pairs]
    means = pair_sums(ys, False)
    ycs = [ys[p] - means[p] * (1.0 / R_HEAD_DIM) for p in pairs]
    vars_ = pair_sums([yc * yc for yc in ycs], False)
    for p, sl in zip(pairs, sls):
        yn = ycs[p] * lax.rsqrt(vars_[p] * (1.0 / R_HEAD_DIM) + R_GN_EPS) * lnw[:, sl] + lnb[:, sl]
        yn = yn + sums[p][L:] * v_ps[p]
        y_ref[:, sl] = (yn * g[:, sl]).astype(y_ref.dtype)


def _rwkv(z3, v_first, p):
    b, t, _ = z3.shape
    L = R_CHUNK
    has_vres = v_first is not None

    def zspec(width, col):
        return pl.BlockSpec((None, L, width), lambda i, c: (i, c, col // width))

    def full(a):
        return pl.BlockSpec(a.shape, lambda i, c: (0,) * a.ndim)

    seq_spec = pl.BlockSpec((None, L, D_BRANCH), lambda i, c: (i, c, 0))
    names = ["mu", "mul", "w0", "w2", "a0", "a2", "g2", "kk", "ka", "rk", "lnw", "lnb"]
    if has_vres:
        names += ["v0", "v1", "v2"]
    consts = [p[n] for n in names]
    z_specs = [zspec(1024, Z_R), zspec(1024, Z_K), zspec(1024, Z_V), zspec(256, Z_LORA)]
    args = [z3, z3, z3, z3]
    if has_vres:
        z_specs.append(seq_spec)
        args.append(v_first)
    y_shape = jax.ShapeDtypeStruct((b, t, D_BRANCH), BF16)
    if has_vres:
        out_shape, out_specs = y_shape, seq_spec
    else:
        out_shape = (y_shape, jax.ShapeDtypeStruct((b, t, D_BRANCH), F32))
        out_specs = (seq_spec, seq_spec)
    return pl.pallas_call(
        functools.partial(_rwkv_kernel, has_vres),
        grid=(b, t // L),
        in_specs=z_specs + [full(a) for a in consts],
        out_specs=out_specs,
        out_shape=out_shape,
        scratch_shapes=[pltpu.VMEM((L + 8, R_COLS), F32),
                        pltpu.VMEM((R_HEADS // 2, 2 * R_HEAD_DIM, 2 * R_HEAD_DIM), F32)],
        compiler_params=_params(("parallel", "arbitrary")),
        name="rwkv7",
    )(*args, *consts)


def _pad_cols(a, n):
    return jnp.pad(a, ((0, 0), (0, n - a.shape[1])))


def _pad_rows(a, n, before=0):
    return jnp.pad(a, ((before, n - a.shape[0] - before), (0, 0)))


def _z_tile_table():
    off = dict(zip(("mq", "mk", "mv", "mo", "mi", "mf", "rz", "gq", "gk", "gv", "gz", "go", "gates"),
                   (0,) + IN_SPLITS))
    r0 = off["rz"]
    segs = [(off["mv"], 1024, 1024), (off["mo"], 1024, 1024),
            (r0, 1024, 1024), (r0 + 1024, 1024, 1024), (r0 + 2048, 1024, 1024),
            (off["gv"], 1024, 1024), (off["go"], 1024, 1024),
            (off["mq"], 1024, 1024), (off["gq"], 1024, 1024),
            (r0 + 3072, 256, 256), (off["mi"], 2 * M_HEADS, LANE), (off["gz"], G_RANK, LANE),
            (off["gates"], N_BRANCH * D_MODEL, N_BRANCH * D_MODEL)]
    start, nvalid = [], []
    for src, valid, width in segs:
        for t in range(width // LANE):
            assert (src + LANE * t) % 8 == 0
            start.append(src + LANE * t)
            nvalid.append(max(0, min(LANE, valid - LANE * t)))
    assert len(start) * LANE == Z_COLS
    return start, nvalid


def _prep_w_in_kernel(start_ref, nvalid_ref, src_ref, o_ref):
    del start_ref
    r = _iota((LANE, LANE), 0)
    c = _iota((LANE, LANE), 1)
    select = ((r == c) & (c < nvalid_ref[pl.program_id(0)])).astype(BF16)
    o_ref[...] = _dotb_tn(src_ref[...], select).astype(o_ref.dtype)


def _prep_w_in(w_all_t, layer):
    n_layers, n_src, d = w_all_t.shape
    start, nvalid = (jnp.asarray(a, jnp.int32) for a in _z_tile_table())
    grid_spec = pltpu.PrefetchScalarGridSpec(
        num_scalar_prefetch=2,
        grid=(Z_COLS // LANE,),
        in_specs=[pl.BlockSpec((pl.Element(LANE), pl.Element(d)), lambda j, s, n: (s[j] * 8, 0))],
        out_specs=pl.BlockSpec((d, LANE), lambda j, s, n: (0, j)),
    )
    return pl.pallas_call(
        _prep_w_in_kernel,
        grid_spec=grid_spec,
        out_shape=jax.ShapeDtypeStruct((d, Z_COLS), BF16),
        compiler_params=_params(("parallel",)),
        name="prep_w_in",
    )((start + layer * n_src) // 8, nvalid, w_all_t.reshape(n_layers * n_src, d))


def _row(a):
    return a.reshape(1, -1).astype(F32)


def _pick_tm(m, cap):
    tm = min(m, cap)
    while m % tm:
        tm //= 2
    return tm


def kernel(x, ffn1_norm, ffn1_w_in, ffn1_w_out, mix_norm, w_in, m_conv, m_i_bias, m_f_bias, m_norm, r_mu, r_w0, r_w2, r_a0, r_a2, r_g2, r_k_k, r_k_a, r_r_k, r_ln_w, r_ln_b, r_v0, r_v1, r_v2, g_gk_up, g_gk_bias, g_norm, w_branch, w_out, ffn2_norm, ffn2_w_in, ffn2_w_out, final_norm):
    bsz, seq, d = x.shape
    m = bsz * seq
    depth = w_in.shape[0]
    xs = x.reshape(m, d)
    tm_big = _pick_tm(m, 2048)
    tm_mid = _pick_tm(m, 1024)
    tm_small = _pick_tm(m, 512)

    def ffn(xs, norm_w, w1_all, w2_all, layer):
        h = _rmsnorm(xs, norm_w, BF16)
        act = _ffn_in(h, w1_all, layer, tm_mid)
        return _matmul_res(act, w2_all, layer, xs, 0.5, tm_small, 512, "ffn_out")

    ffn1_w_out_b = ffn1_w_out.astype(BF16)
    ffn2_w_out_b = ffn2_w_out.astype(BF16)
    w_branch_b = w_branch.astype(BF16)
    w_out_b = w_out.astype(BF16)
    w_in_t = jnp.swapaxes(w_in, 1, 2)

    v_first = None
    for l in range(depth):
        xs = ffn(xs, ffn1_norm[l], ffn1_w_in, ffn1_w_out_b, l)

        h = _rmsnorm(xs, mix_norm[l], BF16)
        z2 = _matmul(h, _prep_w_in(w_in_t, l), tm_big, 512, F32)
        z3 = z2.reshape(bsz, seq, Z_COLS)

        gate_bias = _pad_cols(jnp.concatenate([m_i_bias[l], m_f_bias[l]]).reshape(1, -1), LANE)
        y_m = _mlstm(z3, m_conv[l], gate_bias, _row(m_norm[l]))

        mu = r_mu[l]
        rp = {
            "mu": _row(mu[:3 * D_BRANCH]), "mul": _row(mu[3 * D_BRANCH:]),
            "w0": _row(r_w0[l]), "w2": _pad_rows(r_w2[l], LANE).astype(BF16),
            "a0": _row(r_a0[l]), "a2": _pad_rows(r_a2[l], LANE, before=R_W_RANK).astype(BF16),
            "g2": r_g2[l].astype(BF16),
            "kk": _row(r_k_k[l]), "ka": _row(r_k_a[l]), "rk": _row(r_r_k[l]),
            "lnw": _row(r_ln_w[l]), "lnb": _row(r_ln_b[l]),
        }
        if l == 0:
            y_r, v_first = _rwkv(z3, None, rp)
        else:
            rp["v0"] = _row(r_v0[l - 1])
            rp["v1"] = _pad_cols(r_v1[l - 1], LANE).astype(BF16)
            rp["v2"] = _pad_rows(r_v2[l - 1], LANE).astype(BF16)
            y_r = _rwkv(z3, v_first, rp)

        y_g = _gla(z3, _pad_rows(g_gk_up[l], LANE), _row(g_gk_bias[l]), _row(g_norm[l]))

        merged = _merge(y_m.reshape(m, D_BRANCH), y_r.reshape(m, D_BRANCH), y_g.reshape(m, D_BRANCH),
                        w_branch_b, l, z2, tm_mid)
        xs = _matmul_res(merged, w_out_b, l, xs, 1.0, tm_mid, 512, "mix_out")

        xs = ffn(xs, ffn2_norm[l], ffn2_w_in, ffn2_w_out_b, l)

    out = _rmsnorm(xs, final_norm, F32)
    return out.reshape(bsz, seq, d)
```

```python
import functools

import jax
import jax.numpy as jnp
from jax import lax
from jax.experimental import pallas as pl
from jax.experimental.pallas import tpu as pltpu

F32 = jnp.float32
BF16 = jnp.bfloat16

D_MODEL = 4096
D_BRANCH = 1024
N_BRANCH = 3
D_FF = 11008
EPS = 1e-6
M_HEADS, M_DV, M_DK, M_CONV, M_GATE_CAP = 4, 256, 128, 4, 15.0
R_HEAD_DIM, R_HEADS = 64, 16
R_W_RANK, R_A_RANK, R_G_RANK, R_V_RANK = 64, 64, 128, 32
R_GN_EPS = 64e-5
G_HEADS, G_DV, G_DK, G_RANK, G_TAU = 4, 256, 128, 16, 16.0

M_SIZES = (M_HEADS * M_DK, M_HEADS * M_DK, D_BRANCH, D_BRANCH, M_HEADS, M_HEADS)
R_SIZES = (D_BRANCH, D_BRANCH, D_BRANCH, R_W_RANK, R_A_RANK, R_G_RANK)
G_SIZES = (G_HEADS * G_DK, G_HEADS * G_DK, D_BRANCH, G_RANK, D_BRANCH)
R_COLS = sum(R_SIZES)
IN_SIZES = M_SIZES + (R_COLS,) + G_SIZES + (N_BRANCH * D_MODEL,)


def _cumsplits(sizes):
    out, acc = [], 0
    for s in sizes[:-1]:
        acc += s
        out.append(acc)
    return tuple(out)


IN_SPLITS = _cumsplits(IN_SIZES)
R_SPLITS = _cumsplits(R_SIZES)

LANE = 128
VMEM_LIMIT = 56 * 1024 * 1024
VMEM_LIMIT_FFN_IN = 60 * 1024 * 1024

Z_MV, Z_MO, Z_R, Z_K, Z_V, Z_GV, Z_GO, Z_MQK, Z_GQK = (i * 1024 for i in range(9))
Z_LORA = 9216
Z_MIF = 9472
Z_GZ = 9600
Z_GATES = 9728
Z_COLS = Z_GATES + N_BRANCH * D_MODEL

FF_TN = 256
M_CHUNK = 128
G_CHUNK = 64
G_SUB = 16
R_CHUNK = 64


def _params(sem, vmem=None):
    return pltpu.CompilerParams(dimension_semantics=sem, vmem_limit_bytes=vmem)


def _dotb(a, b):
    return jnp.dot(a.astype(BF16), b.astype(BF16), preferred_element_type=F32)


def _dotb_nt(a, b):
    return lax.dot_general(a.astype(BF16), b.astype(BF16), (((1,), (1,)), ((), ())),
                           preferred_element_type=F32)


def _dotb_tn(a, b):
    return lax.dot_general(a.astype(BF16), b.astype(BF16), (((0,), (0,)), ((), ())),
                           preferred_element_type=F32)


def _dot32(a, b):
    return jnp.dot(a, b, precision=lax.Precision.HIGHEST, preferred_element_type=F32)


def _log_sigmoid(x):
    return jnp.minimum(x, 0.0) - jnp.log1p(jnp.exp(-jnp.abs(x)))


def _sigmoid(x):
    return 0.5 * jnp.tanh(0.5 * x) + 0.5


def _iota(shape, dim):
    return lax.broadcasted_iota(jnp.int32, shape, dim)


def _tril_ones(n):
    return (_iota((n, n), 0) >= _iota((n, n), 1)).astype(F32)


def _rmsnorm_kernel(x_ref, w_ref, o_ref):
    x = x_ref[...]
    ms = jnp.mean(x * x, axis=-1, keepdims=True)
    o_ref[...] = (x * lax.rsqrt(ms + EPS) * w_ref[...]).astype(o_ref.dtype)


def _rmsnorm(x, w, out_dtype, tm=256):
    m, d = x.shape
    return pl.pallas_call(
        _rmsnorm_kernel,
        grid=(m // tm,),
        in_specs=[pl.BlockSpec((tm, d), lambda i: (i, 0)),
                  pl.BlockSpec((1, d), lambda i: (0, 0))],
        out_specs=pl.BlockSpec((tm, d), lambda i: (i, 0)),
        out_shape=jax.ShapeDtypeStruct((m, d), out_dtype),
        compiler_params=_params(("parallel",)),
        name="rmsnorm",
    )(x, w.reshape(1, d))


def _ffn_in_kernel(h_ref, wg_ref, wu_ref, o_ref, wbuf):
    @pl.when(pl.program_id(1) == 0)
    def _():
        wbuf[:, :FF_TN] = wg_ref[...].astype(BF16)
        wbuf[:, FF_TN:] = wu_ref[...].astype(BF16)

    z = jnp.dot(h_ref[...], wbuf[...], preferred_element_type=F32)
    g = z[:, :FF_TN]
    u = z[:, FF_TN:]
    o_ref[...] = (g * _sigmoid(g) * u).astype(o_ref.dtype)


def _ffn_in(h, w_all, layer, tm):
    m, d = h.shape
    n = w_all.shape[2] // 2
    nt = n // FF_TN
    return pl.pallas_call(
        _ffn_in_kernel,
        grid=(nt, m // tm),
        in_specs=[pl.BlockSpec((tm, d), lambda j, i: (i, 0)),
                  pl.BlockSpec((None, d, FF_TN), lambda j, i: (layer, 0, j)),
                  pl.BlockSpec((None, d, FF_TN), lambda j, i: (layer, 0, j + nt))],
        out_specs=pl.BlockSpec((tm, FF_TN), lambda j, i: (i, j)),
        out_shape=jax.ShapeDtypeStruct((m, n), BF16),
        scratch_shapes=[pltpu.VMEM((d, 2 * FF_TN), BF16)],
        compiler_params=_params(("parallel", "arbitrary"), VMEM_LIMIT_FFN_IN),
        name="ffn_in",
    )(h, w_all, w_all)


def _matmul_kernel(a_ref, w_ref, o_ref):
    o_ref[...] = jnp.dot(a_ref[...], w_ref[...], preferred_element_type=F32).astype(o_ref.dtype)


def _matmul(a, w, tm, tn, out_dtype):
    m, k = a.shape
    n = w.shape[1]
    return pl.pallas_call(
        _matmul_kernel,
        grid=(m // tm, n // tn),
        in_specs=[pl.BlockSpec((tm, k), lambda i, j: (i, 0)),
                  pl.BlockSpec((k, tn), lambda i, j: (0, j))],
        out_specs=pl.BlockSpec((tm, tn), lambda i, j: (i, j)),
        out_shape=jax.ShapeDtypeStruct((m, n), out_dtype),
        compiler_params=_params(("parallel", "arbitrary"), VMEM_LIMIT),
        name="mix_in",
    )(a, w)


def _matmul_res_kernel(scale, a_ref, w_ref, x_ref, o_ref):
    y = jnp.dot(a_ref[...], w_ref[...], preferred_element_type=F32)
    o_ref[...] = x_ref[...] + scale * y


def _matmul_res(a, w_all, layer, x, scale, tm, tn, name):
    m, k = a.shape
    n = w_all.shape[2]
    return pl.pallas_call(
        functools.partial(_matmul_res_kernel, scale),
        grid=(m // tm, n // tn),
        in_specs=[pl.BlockSpec((tm, k), lambda i, j: (i, 0)),
                  pl.BlockSpec((None, k, tn), lambda i, j: (layer, 0, j)),
                  pl.BlockSpec((tm, tn), lambda i, j: (i, j))],
        out_specs=pl.BlockSpec((tm, tn), lambda i, j: (i, j)),
        out_shape=jax.ShapeDtypeStruct((m, n), F32),
        compiler_params=_params(("parallel", "arbitrary"), VMEM_LIMIT),
        name=name,
    )(a, w_all, x)


def _merge_kernel(ym_ref, yr_ref, yg_ref, wb_ref, g0_ref, g1_ref, g2_ref, o_ref):
    acc = _sigmoid(g0_ref[...]) * jnp.dot(ym_ref[...], wb_ref[0], preferred_element_type=F32)
    acc += _sigmoid(g1_ref[...]) * jnp.dot(yr_ref[...], wb_ref[1], preferred_element_type=F32)
    acc += _sigmoid(g2_ref[...]) * jnp.dot(yg_ref[...], wb_ref[2], preferred_element_type=F32)
    o_ref[...] = acc.astype(o_ref.dtype)


def _merge(ym, yr, yg, wb_all, layer, z2d, tm, tn=512):
    m = ym.shape[0]
    gate_blk = Z_GATES // tn
    per = D_MODEL // tn
    y_spec = pl.BlockSpec((tm, D_BRANCH), lambda i, j: (i, 0))

    def gate_spec(b):
        return pl.BlockSpec((tm, tn), lambda i, j: (i, gate_blk + b * per + j))

    return pl.pallas_call(
        _merge_kernel,
        grid=(m // tm, D_MODEL // tn),
        in_specs=[y_spec, y_spec, y_spec,
                  pl.BlockSpec((None, N_BRANCH, D_BRANCH, tn), lambda i, j: (layer, 0, 0, j)),
                  gate_spec(0), gate_spec(1), gate_spec(2)],
        out_specs=pl.BlockSpec((tm, tn), lambda i, j: (i, j)),
        out_shape=jax.ShapeDtypeStruct((m, D_MODEL), BF16),
        compiler_params=_params(("parallel", "arbitrary"), VMEM_LIMIT),
        name="merge",
    )(ym, yr, yg, wb_all, z2d, z2d, z2d)


def _mlstm_kernel(qk_ref, v_ref, o_ref, mif_ref, convw_ref, bias_ref, normw_ref, y_ref,
                  qkbuf, cstate, mstate):
    L = M_CHUNK

    @pl.when(pl.program_id(1) == 0)
    def _():
        qkbuf[0:8, :] = jnp.zeros((8, 2 * M_HEADS * M_DK), F32)
        cstate[...] = jnp.zeros_like(cstate)
        mstate[...] = jnp.zeros_like(mstate)

    qkbuf[8:8 + L, :] = qk_ref[...]
    cw = convw_ref[...]
    conv = (cw[0:1] * qkbuf[5:5 + L, :] + cw[1:2] * qkbuf[6:6 + L, :]
            + cw[2:3] * qkbuf[7:7 + L, :] + cw[3:4] * qkbuf[8:8 + L, :])
    qkbuf[0:8, :] = qkbuf[L:L + 8, :]
    qk = conv * _sigmoid(conv)

    capped = M_GATE_CAP * jnp.tanh((mif_ref[...] + bias_ref[...]) * (1.0 / M_GATE_CAP))
    logf = _log_sigmoid(capped)
    bcum = _dot32(_tril_ones(L), logf)
    bcum_t = bcum.T
    capped_t = capped.T

    causal = _iota((L, L), 0) >= _iota((L, L), 1)
    ones_col = (_iota((L, LANE), 1) == 0).astype(F32)
    v_all = v_ref[...]
    o_all = o_ref[...]
    nw = normw_ref[...]

    heads = range(M_HEADS)
    qs = [qk[:, h * M_DK:(h + 1) * M_DK] for h in heads]
    ks = [qk[:, (M_HEADS + h) * M_DK:(M_HEADS + h + 1) * M_DK] * (M_DK ** -0.5) for h in heads]
    v_augs = [jnp.concatenate([v_all[:, h * M_DV:(h + 1) * M_DV], ones_col], axis=1) for h in heads]
    c_prevs = [cstate[h] for h in heads]
    qks = [_dotb_nt(qs[h], ks[h]) for h in heads]
    qcs = [_dotb(qs[h], c_prevs[h]) for h in heads]

    ss, inters, m_ts, kws, decays = [], [], [], [], []
    for h in heads:
        b_col = bcum[:, M_HEADS + h:M_HEADS + h + 1]
        i_col = capped[:, h:h + 1]
        b_row = bcum_t[M_HEADS + h:M_HEADS + h + 1, :]
        i_row = capped_t[h:h + 1, :]
        m_prev = mstate[h:h + 1, 0:1]
        log_d = jnp.where(causal, b_col - b_row + i_row, -jnp.inf)
        log_inter = b_col + m_prev
        m_t = jnp.maximum(log_inter, jnp.max(log_d, axis=-1, keepdims=True))
        ss.append(qks[h] * jnp.exp(log_d - m_t))
        inters.append(jnp.exp(log_inter - m_t))
        m_ts.append(m_t)

        b_last = b_col[L - 1:L, :]
        log_w = b_last - b_col + i_col
        m_new = jnp.maximum(b_last + m_prev, jnp.max(log_w, axis=0, keepdims=True))
        kws.append(ks[h] * jnp.exp(log_w - m_new))
        decays.append(jnp.exp(b_last + m_prev - m_new))
        mstate[h:h + 1, :] = jnp.broadcast_to(m_new, (1, LANE))

    svs = [_dotb(ss[h], v_augs[h]) for h in heads]
    c_upd = [_dotb_tn(kws[h], v_augs[h]) for h in heads]
    for h in heads:
        cstate[h] = decays[h] * c_prevs[h] + c_upd[h]
        numden = svs[h] + inters[h] * qcs[h]
        num = numden[:, :M_DV]
        den = numden[:, M_DV:M_DV + 1]
        hh = num / jnp.maximum(jnp.abs(den), jnp.exp(-m_ts[h]))
        y = hh * lax.rsqrt(jnp.mean(hh * hh, axis=-1, keepdims=True) + EPS)
        y = y * nw[:, h * M_DV:(h + 1) * M_DV]
        y = y * _sigmoid(o_all[:, h * M_DV:(h + 1) * M_DV])
        y_ref[:, h * M_DV:(h + 1) * M_DV] = y.astype(y_ref.dtype)


def _mlstm(z3, conv_w, gate_bias, norm_w):
    b, t, _ = z3.shape
    L = M_CHUNK

    def zspec(width, col):
        return pl.BlockSpec((None, L, width), lambda i, c: (i, c, col // width))

    def full(shape):
        return pl.BlockSpec(shape, lambda i, c: (0,) * len(shape))

    return pl.pallas_call(
        _mlstm_kernel,
        grid=(b, t // L),
        in_specs=[zspec(1024, Z_MQK), zspec(1024, Z_MV), zspec(1024, Z_MO), zspec(LANE, Z_MIF),
                  full((M_CONV, 1024)), full((1, LANE)), full((1, D_BRANCH))],
        out_specs=pl.BlockSpec((None, L, D_BRANCH), lambda i, c: (i, c, 0)),
        out_shape=jax.ShapeDtypeStruct((b, t, D_BRANCH), BF16),
        scratch_shapes=[pltpu.VMEM((L + 8, 1024), F32),
                        pltpu.VMEM((M_HEADS, M_DK, M_DV + LANE), F32),
                        pltpu.VMEM((8, LANE), F32)],
        compiler_params=_params(("parallel", "arbitrary")),
        name="mlstm",
    )(z3, z3, z3, z3, conv_w, gate_bias, norm_w)


def _gla_intra(q, k, bc, col_masks):
    L = q.shape[0]
    lane = _iota((G_SUB, L), 1)
    blocks = []
    for c in range(L // G_SUB):
        r0 = c * G_SUB
        qb = q[r0:r0 + G_SUB]
        bcb = bc[r0:r0 + G_SUB]
        acc = jnp.zeros((G_SUB, L), F32)
        if c > 0:
            ref = bc[r0 - 1:r0]
            qc = qb * jnp.exp(bcb - ref)
            kc = k * jnp.exp(jnp.minimum(ref - bc, 0.0))
            acc = jnp.where(lane < r0, _dotb_nt(qc, kc), 0.0)
        for j in range(G_SUB):
            s = r0 + j
            e = jnp.exp(jnp.minimum(bcb - bc[s:s + 1], 0.0))
            col = jnp.sum(qb * e * k[s:s + 1], axis=-1, keepdims=True)
            acc = jnp.where(col_masks[c][j], col, acc)
        blocks.append(acc)
    return jnp.concatenate(blocks, axis=0)


def _gla_kernel(qk_ref, v_ref, go_ref, gz_ref, gkup_ref, gkb_ref, normw_ref, y_ref, sstate):
    L = G_CHUNK

    @pl.when(pl.program_id(1) == 0)
    def _():
        sstate[...] = jnp.zeros_like(sstate)

    x = _dot32(gz_ref[...], gkup_ref[...]) + gkb_ref[...]
    log_a = _log_sigmoid(x) * (1.0 / G_TAU)
    bc = _dot32(_tril_ones(L), log_a)
    last = bc[L - 1:L, :]
    e_pos = jnp.exp(bc)
    e_tail = jnp.exp(last - bc)
    e_last = jnp.exp(last)

    sub_lane = _iota((G_SUB, L), 1)
    sub_row = _iota((G_SUB, L), 0)
    col_masks = [[(sub_lane == c * G_SUB + j) & (sub_row >= j) for j in range(G_SUB)]
                 for c in range(L // G_SUB)]
    qk = qk_ref[...]
    v_all = v_ref[...]
    go = go_ref[...]
    nw = normw_ref[...]

    heads = range(G_HEADS)
    sls = [slice(h * G_DK, (h + 1) * G_DK) for h in heads]
    vss = [slice(h * G_DV, (h + 1) * G_DV) for h in heads]
    ks = [qk[:, G_HEADS * G_DK + h * G_DK:G_HEADS * G_DK + (h + 1) * G_DK] for h in heads]
    qss = [qk[:, sls[h]] * (G_DK ** -0.5) for h in heads]
    qds = [qss[h] * e_pos[:, sls[h]] for h in heads]
    s_prevs = [sstate[h] for h in heads]
    atts = [_gla_intra(qss[h], ks[h], bc[:, sls[h]], col_masks) for h in heads]
    o_inter = [_dotb_nt(qds[h], s_prevs[h]) for h in heads]
    o_intra = [_dotb(atts[h], v_all[:, vss[h]]) for h in heads]
    s_upd = [_dotb_tn(v_all[:, vss[h]], ks[h] * e_tail[:, sls[h]]) for h in heads]
    for h in heads:
        sstate[h] = s_prevs[h] * e_last[:, sls[h]] + s_upd[h]
        o = o_intra[h] + o_inter[h]
        y = o * lax.rsqrt(jnp.mean(o * o, axis=-1, keepdims=True) + EPS) * nw[:, vss[h]]
        g = go[:, vss[h]]
        y_ref[:, vss[h]] = (y * (g * _sigmoid(g))).astype(y_ref.dtype)


def _gla(z3, gk_up, gk_bias, norm_w):
    b, t, _ = z3.shape
    L = G_CHUNK

    def zspec(width, col):
        return pl.BlockSpec((None, L, width), lambda i, c: (i, c, col // width))

    def full(shape):
        return pl.BlockSpec(shape, lambda i, c: (0,) * len(shape))

    return pl.pallas_call(
        _gla_kernel,
        grid=(b, t // L),
        in_specs=[zspec(1024, Z_GQK), zspec(1024, Z_GV), zspec(1024, Z_GO), zspec(LANE, Z_GZ),
                  full((LANE, G_HEADS * G_DK)), full((1, G_HEADS * G_DK)), full((1, D_BRANCH))],
        out_specs=pl.BlockSpec((None, L, D_BRANCH), lambda i, c: (i, c, 0)),
        out_shape=jax.ShapeDtypeStruct((b, t, D_BRANCH), BF16),
        scratch_shapes=[pltpu.VMEM((G_HEADS, G_DV, G_DK), F32)],
        compiler_params=_params(("parallel", "arbitrary")),
        name="gla",
    )(z3, z3, z3, z3, gk_up, gk_bias, norm_w)


def _rwkv_kernel(has_vres, *refs):
    if has_vres:
        (r_ref, k_ref, v_ref, lora_ref, vfirst_ref, mu_ref, mul_ref, w0_ref, w2_ref, a0_ref, a2_ref,
         g2_ref, kk_ref, ka_ref, rk_ref, lnw_ref, lnb_ref, v0_ref, v1_ref, v2_ref,
         y_ref, zbuf, sstate) = refs
    else:
        (r_ref, k_ref, v_ref, lora_ref, mu_ref, mul_ref, w0_ref, w2_ref, a0_ref, a2_ref,
         g2_ref, kk_ref, ka_ref, rk_ref, lnw_ref, lnb_ref,
         y_ref, vout_ref, zbuf, sstate) = refs
    L = R_CHUNK
    H2 = 2 * R_HEAD_DIM
    n_pairs = R_HEADS // 2

    @pl.when(pl.program_id(1) == 0)
    def _():
        zbuf[0:8, :] = jnp.zeros((8, R_COLS), F32)
        sstate[...] = jnp.zeros_like(sstate)

    zbuf[8:8 + L, 0:1024] = r_ref[...]
    zbuf[8:8 + L, 1024:2048] = k_ref[...]
    zbuf[8:8 + L, 2048:3072] = v_ref[...]
    zbuf[8:8 + L, 3072:R_COLS] = lora_ref[...]
    cur = zbuf[8:8 + L, :]
    prev = zbuf[7:7 + L, :]
    zbuf[0:8, :] = zbuf[L:L + 8, :]
    mu_all = jnp.concatenate([mu_ref[...], mul_ref[...]], axis=1)
    zs = cur + (prev - cur) * mu_all
    r = zs[:, 0:1024]
    k = zs[:, 1024:2048]
    v = zs[:, 2048:3072]
    zwa = zs[:, 3072:3200]
    zg = zs[:, 3200:3328]

    w_log = _log_sigmoid(w0_ref[...] + _dotb(jnp.tanh(zwa), w2_ref[...])) - 0.5
    lw = -jnp.exp(w_log)
    a_lr = _sigmoid(a0_ref[...] + _dotb(zwa, a2_ref[...]))
    g = _dotb(_sigmoid(zg), g2_ref[...])
    if has_vres:
        mix = _sigmoid(v0_ref[...] + _dotb(_dotb(v, v1_ref[...]), v2_ref[...]))
        v = v + (vfirst_ref[...] - v) * mix
    else:
        vout_ref[...] = v

    cum = _dot32(_tril_ones(L), lw)
    cum_last = cum[L - 1:L, :]
    gam = jnp.exp(cum)
    gam_prev = jnp.exp(cum - lw)
    gam_inv = jnp.exp(-cum)
    gam_tail = jnp.exp(cum_last - cum)
    gam_last = jnp.exp(cum_last)

    kk = k * kk_ref[...]
    k_fin = k * (1.0 + (a_lr - 1.0) * ka_ref[...])
    rkr = r * k_fin * rk_ref[...]

    row = _iota((H2, H2), 0)
    col = _iota((H2, H2), 1)
    same_half = (row < R_HEAD_DIM) == (col < R_HEAD_DIM)
    bd = same_half.astype(F32)
    anti = 1.0 - bd
    t_in = row % R_HEAD_DIM
    s_in = col % R_HEAD_DIM
    strict = (s_in < t_in).astype(F32)
    gram_mask = jnp.where(row < R_HEAD_DIM, strict, (s_in <= t_in).astype(F32))
    lane_lo = (_iota((L, H2), 1) < R_HEAD_DIM)
    eye = (row == col).astype(F32)
    ones_bd = bd.astype(BF16)
    lnw = lnw_ref[...]
    lnb = lnb_ref[...]

    pairs = range(n_pairs)
    sls = [slice(p * H2, (p + 1) * H2) for p in pairs]
    lane_lo2 = jnp.concatenate([lane_lo, lane_lo], axis=0)

    def pair_sums(xs, split):
        his = [x.astype(BF16) for x in xs]
        a = [jnp.dot(h, ones_bd, preferred_element_type=F32) for h in his]
        if not split:
            return a
        los = [(x - h.astype(F32)).astype(BF16) for x, h in zip(xs, his)]
        b = [jnp.dot(l, ones_bd, preferred_element_type=F32) for l in los]
        return [x + y for x, y in zip(a, b)]

    v_ps = [v[:, sl] for sl in sls]
    sums = pair_sums([jnp.concatenate([kk[:, sl] * kk[:, sl], rkr[:, sl]], axis=0) for sl in sls], True)
    x_ars, y_bks, y_kbs, bk_hs = [], [], [], []
    for p, sl in zip(pairs, sls):
        kk_n = kk[:, sl] / jnp.maximum(jnp.sqrt(sums[p][0:L]), 1e-12)
        b_p = kk_n * a_lr[:, sl]
        a_t = -kk_n * gam_prev[:, sl]
        r_t = r[:, sl] * gam[:, sl]
        b_t = b_p * gam_inv[:, sl]
        k_t = k_fin[:, sl] * gam_inv[:, sl]
        x_ars.append(jnp.concatenate([a_t, r_t], axis=0))
        y_bks.append(jnp.concatenate([b_t, k_t], axis=0))
        y_kbs.append(jnp.concatenate([k_t, b_t], axis=0))
        bk_hs.append(jnp.concatenate([b_p * gam_tail[:, sl], k_fin[:, sl] * gam_tail[:, sl]], axis=0))

    g1s = [_dotb_nt(jnp.where(lane_lo2, x_ars[p], 0.0), y_bks[p]) * gram_mask for p in pairs]
    g2s = [_dotb_nt(jnp.where(lane_lo2, 0.0, x_ars[p]), y_kbs[p]) * gram_mask for p in pairs]

    pws = [jnp.concatenate([g1s[p][0:L], g2s[p][0:L]], axis=0) * bd for p in pairs]
    t_invs = [eye + pws[p] for p in pairs]
    for _ in range(5):
        pws = [_dotb(pw, pw) for pw in pws]
        t_upd = [_dotb(pws[p], t_invs[p]) for p in pairs]
        t_invs = [t_invs[p] + t_upd[p] for p in pairs]

    m2s = [_dotb(jnp.where(lane_lo2, g2s[p], g1s[p]),
                 jnp.concatenate([v_ps[p], v_ps[p]], axis=0) * anti) for p in pairs]
    s_prevs = [sstate[p] for p in pairs]
    m1s = [_dotb_nt(x_ars[p], s_prevs[p]) for p in pairs]
    rhss = [m1s[p][0:L] + m2s[p][0:L] for p in pairs]
    uxs = [_dotb(t_invs[p], jnp.concatenate([rhss[p], rhss[p]], axis=0) * bd) for p in pairs]
    us = [ux[0:L] + ux[L:] for ux in uxs]
    rbus = [_dotb(jnp.where(lane_lo, g1s[p][L:], g2s[p][L:]),
                  jnp.concatenate([us[p], us[p]], axis=0) * bd) for p in pairs]
    s_upd = [_dotb_tn(jnp.concatenate([us[p], v_ps[p]], axis=0), bk_hs[p]) for p in pairs]
    for p, sl in zip(pairs, sls):
        sstate[p] = s_prevs[p] * gam_last[:, sl] + s_upd[p] * bd

    ys = [m1s[p][L:] + m2s[p][L:] + rbus[p] for p in pairs]
    means = pair_sums(ys, False)
    ycs = [ys[p] - means[p] * (1.0 / R_HEAD_DIM) for p in pairs]
    vars_ = pair_sums([yc * yc for yc in ycs], False)
    for p, sl in zip(pairs, sls):
        yn = ycs[p] * lax.rsqrt(vars_[p] * (1.0 / R_HEAD_DIM) + R_GN_EPS) * lnw[:, sl] + lnb[:, sl]
        yn = yn + sums[p][L:] * v_ps[p]
        y_ref[:, sl] = (yn * g[:, sl]).astype(y_ref.dtype)


def _rwkv(z3, v_first, p):
    b, t, _ = z3.shape
    L = R_CHUNK
    has_vres = v_first is not None

    def zspec(width, col):
        return pl.BlockSpec((None, L, width), lambda i, c: (i, c, col // width))

    def full(a):
        return pl.BlockSpec(a.shape, lambda i, c: (0,) * a.ndim)

    seq_spec = pl.BlockSpec((None, L, D_BRANCH), lambda i, c: (i, c, 0))
    names = ["mu", "mul", "w0", "w2", "a0", "a2", "g2", "kk", "ka", "rk", "lnw", "lnb"]
    if has_vres:
        names += ["v0", "v1", "v2"]
    consts = [p[n] for n in names]
    z_specs = [zspec(1024, Z_R), zspec(1024, Z_K), zspec(1024, Z_V), zspec(256, Z_LORA)]
    args = [z3, z3, z3, z3]
    if has_vres:
        z_specs.append(seq_spec)
        args.append(v_first)
    y_shape = jax.ShapeDtypeStruct((b, t, D_BRANCH), BF16)
    if has_vres:
        out_shape, out_specs = y_shape, seq_spec
    else:
        out_shape = (y_shape, jax.ShapeDtypeStruct((b, t, D_BRANCH), F32))
        out_specs = (seq_spec, seq_spec)
    return pl.pallas_call(
        functools.partial(_rwkv_kernel, has_vres),
        grid=(b, t // L),
        in_specs=z_specs + [full(a) for a in consts],
        out_specs=out_specs,
        out_shape=out_shape,
        scratch_shapes=[pltpu.VMEM((L + 8, R_COLS), F32),
                        pltpu.VMEM((R_HEADS // 2, 2 * R_HEAD_DIM, 2 * R_HEAD_DIM), F32)],
        compiler_params=_params(("parallel", "arbitrary")),
        name="rwkv7",
    )(*args, *consts)


def _pad_cols(a, n):
    return jnp.pad(a, ((0, 0), (0, n - a.shape[1])))


def _pad_rows(a, n, before=0):
    return jnp.pad(a, ((before, n - a.shape[0] - before), (0, 0)))


def _z_tile_table():
    off = dict(zip(("mq", "mk", "mv", "mo", "mi", "mf", "rz", "gq", "gk", "gv", "gz", "go", "gates"),
                   (0,) + IN_SPLITS))
    r0 = off["rz"]
    segs = [(off["mv"], 1024, 1024), (off["mo"], 1024, 1024),
            (r0, 1024, 1024), (r0 + 1024, 1024, 1024), (r0 + 2048, 1024, 1024),
            (off["gv"], 1024, 1024), (off["go"], 1024, 1024),
            (off["mq"], 1024, 1024), (off["gq"], 1024, 1024),
            (r0 + 3072, 256, 256), (off["mi"], 2 * M_HEADS, LANE), (off["gz"], G_RANK, LANE),
            (off["gates"], N_BRANCH * D_MODEL, N_BRANCH * D_MODEL)]
    start, nvalid = [], []
    for src, valid, width in segs:
        for t in range(width // LANE):
            assert (src + LANE * t) % 8 == 0
            start.append(src + LANE * t)
            nvalid.append(max(0, min(LANE, valid - LANE * t)))
    assert len(start) * LANE == Z_COLS
    return start, nvalid


def _prep_w_in_kernel(start_ref, nvalid_ref, src_ref, o_ref):
    del start_ref
    r = _iota((LANE, LANE), 0)
    c = _iota((LANE, LANE), 1)
    select = ((r == c) & (c < nvalid_ref[pl.program_id(0)])).astype(BF16)
    o_ref[...] = _dotb_tn(src_ref[...], select).astype(o_ref.dtype)


def _prep_w_in(w_all_t, layer):
    n_layers, n_src, d = w_all_t.shape
    start, nvalid = (jnp.asarray(a, jnp.int32) for a in _z_tile_table())
    grid_spec = pltpu.PrefetchScalarGridSpec(
        num_scalar_prefetch=2,
        grid=(Z_COLS // LANE,),
        in_specs=[pl.BlockSpec((pl.Element(LANE), pl.Element(d)), lambda j, s, n: (s[j] * 8, 0))],
        out_specs=pl.BlockSpec((d, LANE), lambda j, s, n: (0, j)),
    )
    return pl.pallas_call(
        _prep_w_in_kernel,
        grid_spec=grid_spec,
        out_shape=jax.ShapeDtypeStruct((d, Z_COLS), BF16),
        compiler_params=_params(("parallel",)),
        name="prep_w_in",
    )((start + layer * n_src) // 8, nvalid, w_all_t.reshape(n_layers * n_src, d))


def _row(a):
    return a.reshape(1, -1).astype(F32)


def _pick_tm(m, cap):
    tm = min(m, cap)
    while m % tm:
        tm //= 2
    return tm


def kernel(x, ffn1_norm, ffn1_w_in, ffn1_w_out, mix_norm, w_in, m_conv, m_i_bias, m_f_bias, m_norm, r_mu, r_w0, r_w2, r_a0, r_a2, r_g2, r_k_k, r_k_a, r_r_k, r_ln_w, r_ln_b, r_v0, r_v1, r_v2, g_gk_up, g_gk_bias, g_norm, w_branch, w_out, ffn2_norm, ffn2_w_in, ffn2_w_out, final_norm):
    bsz, seq, d = x.shape
    m = bsz * seq
    depth = w_in.shape[0]
    xs = x.reshape(m, d)
    tm_big = _pick_tm(m, 2048)
    tm_mid = _pick_tm(m, 1024)
    tm_small = _pick_tm(m, 512)

    def ffn(xs, norm_w, w1_all, w2_all, layer):
        h = _rmsnorm(xs, norm_w, BF16)
        act = _ffn_in(h, w1_all, layer, tm_big)
        return _matmul_res(act, w2_all, layer, xs, 0.5, tm_small, 512, "ffn_out")

    ffn1_w_out_b = ffn1_w_out.astype(BF16)
    ffn2_w_out_b = ffn2_w_out.astype(BF16)
    w_branch_b = w_branch.astype(BF16)
    w_out_b = w_out.astype(BF16)
    w_in_t = jnp.swapaxes(w_in, 1, 2)

    v_first = None
    for l in range(depth):
        xs = ffn(xs, ffn1_norm[l], ffn1_w_in, ffn1_w_out_b, l)

        h = _rmsnorm(xs, mix_norm[l], BF16)
        z2 = _matmul(h, _prep_w_in(w_in_t, l), tm_big, 512, F32)
        z3 = z2.reshape(bsz, seq, Z_COLS)

        gate_bias = _pad_cols(jnp.concatenate([m_i_bias[l], m_f_bias[l]]).reshape(1, -1), LANE)
        y_m = _mlstm(z3, m_conv[l], gate_bias, _row(m_norm[l]))

        mu = r_mu[l]
        rp = {
            "mu": _row(mu[:3 * D_BRANCH]), "mul": _row(mu[3 * D_BRANCH:]),
            "w0": _row(r_w0[l]), "w2": _pad_rows(r_w2[l], LANE).astype(BF16),
            "a0": _row(r_a0[l]), "a2": _pad_rows(r_a2[l], LANE, before=R_W_RANK).astype(BF16),
            "g2": r_g2[l].astype(BF16),
            "kk": _row(r_k_k[l]), "ka": _row(r_k_a[l]), "rk": _row(r_r_k[l]),
            "lnw": _row(r_ln_w[l]), "lnb": _row(r_ln_b[l]),
        }
        if l == 0:
            y_r, v_first = _rwkv(z3, None, rp)
        else:
            rp["v0"] = _row(r_v0[l - 1])
            rp["v1"] = _pad_cols(r_v1[l - 1], LANE).astype(BF16)
            rp["v2"] = _pad_rows(r_v2[l - 1], LANE).astype(BF16)
            y_r = _rwkv(z3, v_first, rp)

        y_g = _gla(z3, _pad_rows(g_gk_up[l], LANE), _row(g_gk_bias[l]), _row(g_norm[l]))

        merged = _merge(y_m.reshape(m, D_BRANCH), y_r.reshape(m, D_BRANCH), y_g.reshape(m, D_BRANCH),
                        w_branch_b, l, z2, tm_mid)
        xs = _matmul_res(merged, w_out_b, l, xs, 1.0, tm_mid, 512, "mix_out")

        xs = ffn(xs, ffn2_norm[l], ffn2_w_in, ffn2_w_out_b, l)

    out = _rmsnorm(xs, final_norm, F32)
    return out.reshape(bsz, seq, d)
```

```python
import functools

import jax
import jax.numpy as jnp
from jax import lax
from jax.experimental import pallas as pl
from jax.experimental.pallas import tpu as pltpu

F32 = jnp.float32
BF16 = jnp.bfloat16

D_MODEL = 4096
D_BRANCH = 1024
N_BRANCH = 3
D_FF = 11008
EPS = 1e-6
M_HEADS, M_DV, M_DK, M_CONV, M_GATE_CAP = 4, 256, 128, 4, 15.0
R_HEAD_DIM, R_HEADS = 64, 16
R_W_RANK, R_A_RANK, R_G_RANK, R_V_RANK = 64, 64, 128, 32
R_GN_EPS = 64e-5
G_HEADS, G_DV, G_DK, G_RANK, G_TAU = 4, 256, 128, 16, 16.0

M_SIZES = (M_HEADS * M_DK, M_HEADS * M_DK, D_BRANCH, D_BRANCH, M_HEADS, M_HEADS)
R_SIZES = (D_BRANCH, D_BRANCH, D_BRANCH, R_W_RANK, R_A_RANK, R_G_RANK)
G_SIZES = (G_HEADS * G_DK, G_HEADS * G_DK, D_BRANCH, G_RANK, D_BRANCH)
R_COLS = sum(R_SIZES)
IN_SIZES = M_SIZES + (R_COLS,) + G_SIZES + (N_BRANCH * D_MODEL,)


def _cumsplits(sizes):
    out, acc = [], 0
    for s in sizes[:-1]:
        acc += s
        out.append(acc)
    return tuple(out)


IN_SPLITS = _cumsplits(IN_SIZES)
R_SPLITS = _cumsplits(R_SIZES)

LANE = 128
VMEM_LIMIT = 56 * 1024 * 1024
VMEM_LIMIT_FFN_IN = 60 * 1024 * 1024

Z_MV, Z_MO, Z_R, Z_K, Z_V, Z_GV, Z_GO, Z_MQK, Z_GQK = (i * 1024 for i in range(9))
Z_LORA = 9216
Z_MIF = 9472
Z_GZ = 9600
Z_GATES = 9728
Z_COLS = Z_GATES + N_BRANCH * D_MODEL

FF_TN = 256
M_CHUNK = 128
G_CHUNK = 64
G_SUB = 16
R_CHUNK = 64


def _params(sem, vmem=None):
    return pltpu.CompilerParams(dimension_semantics=sem, vmem_limit_bytes=vmem)


def _dotb(a, b):
    return jnp.dot(a.astype(BF16), b.astype(BF16), preferred_element_type=F32)


def _dotb_nt(a, b):
    return lax.dot_general(a.astype(BF16), b.astype(BF16), (((1,), (1,)), ((), ())),
                           preferred_element_type=F32)


def _dotb_tn(a, b):
    return lax.dot_general(a.astype(BF16), b.astype(BF16), (((0,), (0,)), ((), ())),
                           preferred_element_type=F32)


def _dot32(a, b):
    return jnp.dot(a, b, precision=lax.Precision.HIGHEST, preferred_element_type=F32)


def _log_sigmoid(x):
    return jnp.minimum(x, 0.0) - jnp.log1p(jnp.exp(-jnp.abs(x)))


def _sigmoid(x):
    return 0.5 * jnp.tanh(0.5 * x) + 0.5


def _iota(shape, dim):
    return lax.broadcasted_iota(jnp.int32, shape, dim)


def _tril_ones(n):
    return (_iota((n, n), 0) >= _iota((n, n), 1)).astype(F32)


def _rmsnorm_kernel(x_ref, w_ref, o_ref):
    x = x_ref[...]
    ms = jnp.mean(x * x, axis=-1, keepdims=True)
    o_ref[...] = (x * lax.rsqrt(ms + EPS) * w_ref[...]).astype(o_ref.dtype)


def _rmsnorm(x, w, out_dtype, tm=256):
    m, d = x.shape
    return pl.pallas_call(
        _rmsnorm_kernel,
        grid=(m // tm,),
        in_specs=[pl.BlockSpec((tm, d), lambda i: (i, 0)),
                  pl.BlockSpec((1, d), lambda i: (0, 0))],
        out_specs=pl.BlockSpec((tm, d), lambda i: (i, 0)),
        out_shape=jax.ShapeDtypeStruct((m, d), out_dtype),
        compiler_params=_params(("parallel",)),
        name="rmsnorm",
    )(x, w.reshape(1, d))


def _ffn_in_kernel(h_ref, wg_ref, wu_ref, o_ref, wbuf):
    @pl.when(pl.program_id(1) == 0)
    def _():
        wbuf[:, :FF_TN] = wg_ref[...].astype(BF16)
        wbuf[:, FF_TN:] = wu_ref[...].astype(BF16)

    z = jnp.dot(h_ref[...], wbuf[...], preferred_element_type=F32)
    g = z[:, :FF_TN]
    u = z[:, FF_TN:]
    o_ref[...] = (g * _sigmoid(g) * u).astype(o_ref.dtype)


def _ffn_in(h, w_all, layer, tm):
    m, d = h.shape
    n = w_all.shape[2] // 2
    nt = n // FF_TN
    return pl.pallas_call(
        _ffn_in_kernel,
        grid=(nt, m // tm),
        in_specs=[pl.BlockSpec((tm, d), lambda j, i: (i, 0)),
                  pl.BlockSpec((None, d, FF_TN), lambda j, i: (layer, 0, j)),
                  pl.BlockSpec((None, d, FF_TN), lambda j, i: (layer, 0, j + nt))],
        out_specs=pl.BlockSpec((tm, FF_TN), lambda j, i: (i, j)),
        out_shape=jax.ShapeDtypeStruct((m, n), BF16),
        scratch_shapes=[pltpu.VMEM((d, 2 * FF_TN), BF16)],
        compiler_params=_params(("parallel", "arbitrary"), VMEM_LIMIT_FFN_IN),
        name="ffn_in",
    )(h, w_all, w_all)


def _matmul_kernel(a_ref, w_ref, o_ref):
    o_ref[...] = jnp.dot(a_ref[...], w_ref[...], preferred_element_type=F32).astype(o_ref.dtype)


def _matmul(a, w, tm, tn, out_dtype):
    m, k = a.shape
    n = w.shape[1]
    return pl.pallas_call(
        _matmul_kernel,
        grid=(m // tm, n // tn),
        in_specs=[pl.BlockSpec((tm, k), lambda i, j: (i, 0)),
                  pl.BlockSpec((k, tn), lambda i, j: (0, j))],
        out_specs=pl.BlockSpec((tm, tn), lambda i, j: (i, j)),
        out_shape=jax.ShapeDtypeStruct((m, n), out_dtype),
        compiler_params=_params(("parallel", "arbitrary"), VMEM_LIMIT),
        name="mix_in",
    )(a, w)


def _norm_matmul_kernel(x_ref, nw_ref, w_ref, o_ref, hbuf):
    @pl.when(pl.program_id(1) == 0)
    def _():
        x = x_ref[...]
        ms = jnp.mean(x * x, axis=-1, keepdims=True)
        hbuf[...] = (x * lax.rsqrt(ms + EPS) * nw_ref[...]).astype(BF16)

    o_ref[...] = jnp.dot(hbuf[...], w_ref[...], preferred_element_type=F32).astype(o_ref.dtype)


def _norm_matmul(x, norm_w, w, tm, tn, out_dtype):
    m, k = x.shape
    n = w.shape[1]
    return pl.pallas_call(
        _norm_matmul_kernel,
        grid=(m // tm, n // tn),
        in_specs=[pl.BlockSpec((tm, k), lambda i, j: (i, 0), pipeline_mode=pl.Buffered(1)),
                  pl.BlockSpec((1, k), lambda i, j: (0, 0)),
                  pl.BlockSpec((k, tn), lambda i, j: (0, j))],
        out_specs=pl.BlockSpec((tm, tn), lambda i, j: (i, j)),
        out_shape=jax.ShapeDtypeStruct((m, n), out_dtype),
        scratch_shapes=[pltpu.VMEM((tm, k), BF16)],
        compiler_params=_params(("parallel", "arbitrary"), VMEM_LIMIT_FFN_IN),
        name="mix_in",
    )(x, norm_w.reshape(1, k), w)


def _matmul_res_kernel(scale, a_ref, w_ref, x_ref, o_ref):
    y = jnp.dot(a_ref[...], w_ref[...], preferred_element_type=F32)
    o_ref[...] = x_ref[...] + scale * y


def _matmul_res(a, w_all, layer, x, scale, tm, tn, name):
    m, k = a.shape
    n = w_all.shape[2]
    return pl.pallas_call(
        functools.partial(_matmul_res_kernel, scale),
        grid=(m // tm, n // tn),
        in_specs=[pl.BlockSpec((tm, k), lambda i, j: (i, 0)),
                  pl.BlockSpec((None, k, tn), lambda i, j: (layer, 0, j)),
                  pl.BlockSpec((tm, tn), lambda i, j: (i, j))],
        out_specs=pl.BlockSpec((tm, tn), lambda i, j: (i, j)),
        out_shape=jax.ShapeDtypeStruct((m, n), F32),
        compiler_params=_params(("parallel", "arbitrary"), VMEM_LIMIT),
        name=name,
    )(a, w_all, x)


def _merge_kernel(ym_ref, yr_ref, yg_ref, wb_ref, g0_ref, g1_ref, g2_ref, o_ref):
    acc = _sigmoid(g0_ref[...]) * jnp.dot(ym_ref[...], wb_ref[0], preferred_element_type=F32)
    acc += _sigmoid(g1_ref[...]) * jnp.dot(yr_ref[...], wb_ref[1], preferred_element_type=F32)
    acc += _sigmoid(g2_ref[...]) * jnp.dot(yg_ref[...], wb_ref[2], preferred_element_type=F32)
    o_ref[...] = acc.astype(o_ref.dtype)


def _merge(ym, yr, yg, wb_all, layer, z2d, tm, tn=512):
    m = ym.shape[0]
    gate_blk = Z_GATES // tn
    per = D_MODEL // tn
    y_spec = pl.BlockSpec((tm, D_BRANCH), lambda i, j: (i, 0))

    def gate_spec(b):
        return pl.BlockSpec((tm, tn), lambda i, j: (i, gate_blk + b * per + j))

    return pl.pallas_call(
        _merge_kernel,
        grid=(m // tm, D_MODEL // tn),
        in_specs=[y_spec, y_spec, y_spec,
                  pl.BlockSpec((None, N_BRANCH, D_BRANCH, tn), lambda i, j: (layer, 0, 0, j)),
                  gate_spec(0), gate_spec(1), gate_spec(2)],
        out_specs=pl.BlockSpec((tm, tn), lambda i, j: (i, j)),
        out_shape=jax.ShapeDtypeStruct((m, D_MODEL), BF16),
        compiler_params=_params(("parallel", "arbitrary"), VMEM_LIMIT),
        name="merge",
    )(ym, yr, yg, wb_all, z2d, z2d, z2d)


def _mlstm_kernel(qk_ref, v_ref, o_ref, mif_ref, convw_ref, bias_ref, normw_ref, y_ref,
                  qkbuf, cstate, mstate):
    L = M_CHUNK

    @pl.when(pl.program_id(1) == 0)
    def _():
        qkbuf[0:8, :] = jnp.zeros((8, 2 * M_HEADS * M_DK), F32)
        cstate[...] = jnp.zeros_like(cstate)
        mstate[...] = jnp.zeros_like(mstate)

    qkbuf[8:8 + L, :] = qk_ref[...]
    cw = convw_ref[...]
    conv = (cw[0:1] * qkbuf[5:5 + L, :] + cw[1:2] * qkbuf[6:6 + L, :]
            + cw[2:3] * qkbuf[7:7 + L, :] + cw[3:4] * qkbuf[8:8 + L, :])
    qkbuf[0:8, :] = qkbuf[L:L + 8, :]
    qk = conv * _sigmoid(conv)

    capped = M_GATE_CAP * jnp.tanh((mif_ref[...] + bias_ref[...]) * (1.0 / M_GATE_CAP))
    logf = _log_sigmoid(capped)
    bcum = _dot32(_tril_ones(L), logf)
    bcum_t = bcum.T
    capped_t = capped.T

    causal = _iota((L, L), 0) >= _iota((L, L), 1)
    ones_col = (_iota((L, LANE), 1) == 0).astype(F32)
    v_all = v_ref[...]
    o_all = o_ref[...]
    nw = normw_ref[...]

    heads = range(M_HEADS)
    qs = [qk[:, h * M_DK:(h + 1) * M_DK] for h in heads]
    ks = [qk[:, (M_HEADS + h) * M_DK:(M_HEADS + h + 1) * M_DK] * (M_DK ** -0.5) for h in heads]
    v_augs = [jnp.concatenate([v_all[:, h * M_DV:(h + 1) * M_DV], ones_col], axis=1) for h in heads]
    c_prevs = [cstate[h] for h in heads]
    qks = [_dotb_nt(qs[h], ks[h]) for h in heads]
    qcs = [_dotb(qs[h], c_prevs[h]) for h in heads]

    ss, inters, m_ts, kws, decays = [], [], [], [], []
    for h in heads:
        b_col = bcum[:, M_HEADS + h:M_HEADS + h + 1]
        i_col = capped[:, h:h + 1]
        b_row = bcum_t[M_HEADS + h:M_HEADS + h + 1, :]
        i_row = capped_t[h:h + 1, :]
        m_prev = mstate[h:h + 1, 0:1]
        log_d = jnp.where(causal, b_col - b_row + i_row, -jnp.inf)
        log_inter = b_col + m_prev
        m_t = jnp.maximum(log_inter, jnp.max(log_d, axis=-1, keepdims=True))
        ss.append(qks[h] * jnp.exp(log_d - m_t))
        inters.append(jnp.exp(log_inter - m_t))
        m_ts.append(m_t)

        b_last = b_col[L - 1:L, :]
        log_w = b_last - b_col + i_col
        m_new = jnp.maximum(b_last + m_prev, jnp.max(log_w, axis=0, keepdims=True))
        kws.append(ks[h] * jnp.exp(log_w - m_new))
        decays.append(jnp.exp(b_last + m_prev - m_new))
        mstate[h:h + 1, :] = jnp.broadcast_to(m_new, (1, LANE))

    svs = [_dotb(ss[h], v_augs[h]) for h in heads]
    c_upd = [_dotb_tn(kws[h], v_augs[h]) for h in heads]
    for h in heads:
        cstate[h] = decays[h] * c_prevs[h] + c_upd[h]
        numden = svs[h] + inters[h] * qcs[h]
        num = numden[:, :M_DV]
        den = numden[:, M_DV:M_DV + 1]
        hh = num / jnp.maximum(jnp.abs(den), jnp.exp(-m_ts[h]))
        y = hh * lax.rsqrt(jnp.mean(hh * hh, axis=-1, keepdims=True) + EPS)
        y = y * nw[:, h * M_DV:(h + 1) * M_DV]
        y = y * _sigmoid(o_all[:, h * M_DV:(h + 1) * M_DV])
        y_ref[:, h * M_DV:(h + 1) * M_DV] = y.astype(y_ref.dtype)


def _mlstm(z3, conv_w, gate_bias, norm_w):
    b, t, _ = z3.shape
    L = M_CHUNK

    def zspec(width, col):
        return pl.BlockSpec((None, L, width), lambda i, c: (i, c, col // width))

    def full(shape):
        return pl.BlockSpec(shape, lambda i, c: (0,) * len(shape))

    return pl.pallas_call(
        _mlstm_kernel,
        grid=(b, t // L),
        in_specs=[zspec(1024, Z_MQK), zspec(1024, Z_MV), zspec(1024, Z_MO), zspec(LANE, Z_MIF),
                  full((M_CONV, 1024)), full((1, LANE)), full((1, D_BRANCH))],
        out_specs=pl.BlockSpec((None, L, D_BRANCH), lambda i, c: (i, c, 0)),
        out_shape=jax.ShapeDtypeStruct((b, t, D_BRANCH), BF16),
        scratch_shapes=[pltpu.VMEM((L + 8, 1024), F32),
                        pltpu.VMEM((M_HEADS, M_DK, M_DV + LANE), F32),
                        pltpu.VMEM((8, LANE), F32)],
        compiler_params=_params(("parallel", "arbitrary")),
        name="mlstm",
    )(z3, z3, z3, z3, conv_w, gate_bias, norm_w)


def _gla_intra(q, k, bc, col_masks):
    L = q.shape[0]
    lane = _iota((G_SUB, L), 1)
    blocks = []
    for c in range(L // G_SUB):
        r0 = c * G_SUB
        qb = q[r0:r0 + G_SUB]
        bcb = bc[r0:r0 + G_SUB]
        acc = jnp.zeros((G_SUB, L), F32)
        if c > 0:
            ref = bc[r0 - 1:r0]
            qc = qb * jnp.exp(bcb - ref)
            kc = k * jnp.exp(jnp.minimum(ref - bc, 0.0))
            acc = jnp.where(lane < r0, _dotb_nt(qc, kc), 0.0)
        for j in range(G_SUB):
            s = r0 + j
            e = jnp.exp(jnp.minimum(bcb - bc[s:s + 1], 0.0))
            col = jnp.sum(qb * e * k[s:s + 1], axis=-1, keepdims=True)
            acc = jnp.where(col_masks[c][j], col, acc)
        blocks.append(acc)
    return jnp.concatenate(blocks, axis=0)


def _gla_kernel(qk_ref, v_ref, go_ref, gz_ref, gkup_ref, gkb_ref, normw_ref, y_ref, sstate):
    L = G_CHUNK

    @pl.when(pl.program_id(1) == 0)
    def _():
        sstate[...] = jnp.zeros_like(sstate)

    x = _dot32(gz_ref[...], gkup_ref[...]) + gkb_ref[...]
    log_a = _log_sigmoid(x) * (1.0 / G_TAU)
    bc = _dot32(_tril_ones(L), log_a)
    last = bc[L - 1:L, :]
    e_pos = jnp.exp(bc)
    e_tail = jnp.exp(last - bc)
    e_last = jnp.exp(last)

    sub_lane = _iota((G_SUB, L), 1)
    sub_row = _iota((G_SUB, L), 0)
    col_masks = [[(sub_lane == c * G_SUB + j) & (sub_row >= j) for j in range(G_SUB)]
                 for c in range(L // G_SUB)]
    qk = qk_ref[...]
    v_all = v_ref[...]
    go = go_ref[...]
    nw = normw_ref[...]

    heads = range(G_HEADS)
    sls = [slice(h * G_DK, (h + 1) * G_DK) for h in heads]
    vss = [slice(h * G_DV, (h + 1) * G_DV) for h in heads]
    ks = [qk[:, G_HEADS * G_DK + h * G_DK:G_HEADS * G_DK + (h + 1) * G_DK] for h in heads]
    qss = [qk[:, sls[h]] * (G_DK ** -0.5) for h in heads]
    qds = [qss[h] * e_pos[:, sls[h]] for h in heads]
    s_prevs = [sstate[h] for h in heads]
    atts = [_gla_intra(qss[h], ks[h], bc[:, sls[h]], col_masks) for h in heads]
    o_inter = [_dotb_nt(qds[h], s_prevs[h]) for h in heads]
    o_intra = [_dotb(atts[h], v_all[:, vss[h]]) for h in heads]
    s_upd = [_dotb_tn(v_all[:, vss[h]], ks[h] * e_tail[:, sls[h]]) for h in heads]
    for h in heads:
        sstate[h] = s_prevs[h] * e_last[:, sls[h]] + s_upd[h]
        o = o_intra[h] + o_inter[h]
        y = o * lax.rsqrt(jnp.mean(o * o, axis=-1, keepdims=True) + EPS) * nw[:, vss[h]]
        g = go[:, vss[h]]
        y_ref[:, vss[h]] = (y * (g * _sigmoid(g))).astype(y_ref.dtype)


def _gla(z3, gk_up, gk_bias, norm_w):
    b, t, _ = z3.shape
    L = G_CHUNK

    def zspec(width, col):
        return pl.BlockSpec((None, L, width), lambda i, c: (i, c, col // width))

    def full(shape):
        return pl.BlockSpec(shape, lambda i, c: (0,) * len(shape))

    return pl.pallas_call(
        _gla_kernel,
        grid=(b, t // L),
        in_specs=[zspec(1024, Z_GQK), zspec(1024, Z_GV), zspec(1024, Z_GO), zspec(LANE, Z_GZ),
                  full((LANE, G_HEADS * G_DK)), full((1, G_HEADS * G_DK)), full((1, D_BRANCH))],
        out_specs=pl.BlockSpec((None, L, D_BRANCH), lambda i, c: (i, c, 0)),
        out_shape=jax.ShapeDtypeStruct((b, t, D_BRANCH), BF16),
        scratch_shapes=[pltpu.VMEM((G_HEADS, G_DV, G_DK), F32)],
        compiler_params=_params(("parallel", "arbitrary")),
        name="gla",
    )(z3, z3, z3, z3, gk_up, gk_bias, norm_w)


def _rwkv_kernel(has_vres, *refs):
    if has_vres:
        (r_ref, k_ref, v_ref, lora_ref, vfirst_ref, mu_ref, mul_ref, w0_ref, w2_ref, a0_ref, a2_ref,
         g2_ref, kk_ref, ka_ref, rk_ref, lnw_ref, lnb_ref, v0_ref, v1_ref, v2_ref,
         y_ref, zbuf, sstate) = refs
    else:
        (r_ref, k_ref, v_ref, lora_ref, mu_ref, mul_ref, w0_ref, w2_ref, a0_ref, a2_ref,
         g2_ref, kk_ref, ka_ref, rk_ref, lnw_ref, lnb_ref,
         y_ref, vout_ref, zbuf, sstate) = refs
    L = R_CHUNK
    H2 = 2 * R_HEAD_DIM
    n_pairs = R_HEADS // 2

    @pl.when(pl.program_id(1) == 0)
    def _():
        zbuf[0:8, :] = jnp.zeros((8, R_COLS), F32)
        sstate[...] = jnp.zeros_like(sstate)

    zbuf[8:8 + L, 0:1024] = r_ref[...]
    zbuf[8:8 + L, 1024:2048] = k_ref[...]
    zbuf[8:8 + L, 2048:3072] = v_ref[...]
    zbuf[8:8 + L, 3072:R_COLS] = lora_ref[...]
    cur = zbuf[8:8 + L, :]
    prev = zbuf[7:7 + L, :]
    zbuf[0:8, :] = zbuf[L:L + 8, :]
    mu_all = jnp.concatenate([mu_ref[...], mul_ref[...]], axis=1)
    zs = cur + (prev - cur) * mu_all
    r = zs[:, 0:1024]
    k = zs[:, 1024:2048]
    v = zs[:, 2048:3072]
    zwa = zs[:, 3072:3200]
    zg = zs[:, 3200:3328]

    w_log = _log_sigmoid(w0_ref[...] + _dotb(jnp.tanh(zwa), w2_ref[...])) - 0.5
    lw = -jnp.exp(w_log)
    a_lr = _sigmoid(a0_ref[...] + _dotb(zwa, a2_ref[...]))
    g = _dotb(_sigmoid(zg), g2_ref[...])
    if has_vres:
        mix = _sigmoid(v0_ref[...] + _dotb(_dotb(v, v1_ref[...]), v2_ref[...]))
        v = v + (vfirst_ref[...] - v) * mix
    else:
        vout_ref[...] = v

    cum = _dot32(_tril_ones(L), lw)
    cum_last = cum[L - 1:L, :]
    gam = jnp.exp(cum)
    gam_prev = jnp.exp(cum - lw)
    gam_inv = jnp.exp(-cum)
    gam_tail = jnp.exp(cum_last - cum)
    gam_last = jnp.exp(cum_last)

    kk = k * kk_ref[...]
    k_fin = k * (1.0 + (a_lr - 1.0) * ka_ref[...])
    rkr = r * k_fin * rk_ref[...]

    row = _iota((H2, H2), 0)
    col = _iota((H2, H2), 1)
    same_half = (row < R_HEAD_DIM) == (col < R_HEAD_DIM)
    bd = same_half.astype(F32)
    anti = 1.0 - bd
    t_in = row % R_HEAD_DIM
    s_in = col % R_HEAD_DIM
    strict = (s_in < t_in).astype(F32)
    gram_mask = jnp.where(row < R_HEAD_DIM, strict, (s_in <= t_in).astype(F32))
    lane_lo = (_iota((L, H2), 1) < R_HEAD_DIM)
    eye = (row == col).astype(F32)
    ones_bd = bd.astype(BF16)
    lnw = lnw_ref[...]
    lnb = lnb_ref[...]

    pairs = range(n_pairs)
    sls = [slice(p * H2, (p + 1) * H2) for p in pairs]
    lane_lo2 = jnp.concatenate([lane_lo, lane_lo], axis=0)

    def pair_sums(xs, split):
        his = [x.astype(BF16) for x in xs]
        a = [jnp.dot(h, ones_bd, preferred_element_type=F32) for h in his]
        if not split:
            return a
        los = [(x - h.astype(F32)).astype(BF16) for x, h in zip(xs, his)]
        b = [jnp.dot(l, ones_bd, preferred_element_type=F32) for l in los]
        return [x + y for x, y in zip(a, b)]

    v_ps = [v[:, sl] for sl in sls]
    sums = pair_sums([jnp.concatenate([kk[:, sl] * kk[:, sl], rkr[:, sl]], axis=0) for sl in sls], True)
    x_ars, y_bks, y_kbs, bk_hs = [], [], [], []
    for p, sl in zip(pairs, sls):
        kk_n = kk[:, sl] / jnp.maximum(jnp.sqrt(sums[p][0:L]), 1e-12)
        b_p = kk_n * a_lr[:, sl]
        a_t = -kk_n * gam_prev[:, sl]
        r_t = r[:, sl] * gam[:, sl]
        b_t = b_p * gam_inv[:, sl]
        k_t = k_fin[:, sl] * gam_inv[:, sl]
        x_ars.append(jnp.concatenate([a_t, r_t], axis=0))
        y_bks.append(jnp.concatenate([b_t, k_t], axis=0))
        y_kbs.append(jnp.concatenate([k_t, b_t], axis=0))
        bk_hs.append(jnp.concatenate([b_p * gam_tail[:, sl], k_fin[:, sl] * gam_tail[:, sl]], axis=0))

    g1s = [_dotb_nt(jnp.where(lane_lo2, x_ars[p], 0.0), y_bks[p]) * gram_mask for p in pairs]
    g2s = [_dotb_nt(jnp.where(lane_lo2, 0.0, x_ars[p]), y_kbs[p]) * gram_mask for p in pairs]

    pws = [jnp.concatenate([g1s[p][0:L], g2s[p][0:L]], axis=0) * bd for p in pairs]
    t_invs = [eye + pws[p] for p in pairs]
    for _ in range(5):
        pws = [_dotb(pw, pw) for pw in pws]
        t_upd = [_dotb(pws[p], t_invs[p]) for p in pairs]
        t_invs = [t_invs[p] + t_upd[p] for p in pairs]

    m2s = [_dotb(jnp.where(lane_lo2, g2s[p], g1s[p]),
                 jnp.concatenate([v_ps[p], v_ps[p]], axis=0) * anti) for p in pairs]
    s_prevs = [sstate[p] for p in pairs]
    m1s = [_dotb_nt(x_ars[p], s_prevs[p]) for p in pairs]
    rhss = [m1s[p][0:L] + m2s[p][0:L] for p in pairs]
    uxs = [_dotb(t_invs[p], jnp.concatenate([rhss[p], rhss[p]], axis=0) * bd) for p in pairs]
    us = [ux[0:L] + ux[L:] for ux in uxs]
    rbus = [_dotb(jnp.where(lane_lo, g1s[p][L:], g2s[p][L:]),
                  jnp.concatenate([us[p], us[p]], axis=0) * bd) for p in pairs]
    s_upd = [_dotb_tn(jnp.concatenate([us[p], v_ps[p]], axis=0), bk_hs[p]) for p in pairs]
    for p, sl in zip(pairs, sls):
        sstate[p] = s_prevs[p] * gam_last[:, sl] + s_upd[p] * bd

    ys = [m1s[p][L:] + m2s[p][L:] + rbus[p] for p in pairs]
    means = pair_sums(ys, False)
    ycs = [ys[p] - means[p] * (1.0 / R_HEAD_DIM) for p in pairs]
    vars_ = pair_sums([yc * yc for yc in ycs], False)
    for p, sl in zip(pairs, sls):
        yn = ycs[p] * lax.rsqrt(vars_[p] * (1.0 / R_HEAD_DIM) + R_GN_EPS) * lnw[:, sl] + lnb[:, sl]
        yn = yn + sums[p][L:] * v_ps[p]
        y_ref[:, sl] = (yn * g[:, sl]).astype(y_ref.dtype)


def _rwkv(z3, v_first, p):
    b, t, _ = z3.shape
    L = R_CHUNK
    has_vres = v_first is not None

    def zspec(width, col):
        return pl.BlockSpec((None, L, width), lambda i, c: (i, c, col // width))

    def full(a):
        return pl.BlockSpec(a.shape, lambda i, c: (0,) * a.ndim)

    seq_spec = pl.BlockSpec((None, L, D_BRANCH), lambda i, c: (i, c, 0))
    names = ["mu", "mul", "w0", "w2", "a0", "a2", "g2", "kk", "ka", "rk", "lnw", "lnb"]
    if has_vres:
        names += ["v0", "v1", "v2"]
    consts = [p[n] for n in names]
    z_specs = [zspec(1024, Z_R), zspec(1024, Z_K), zspec(1024, Z_V), zspec(256, Z_LORA)]
    args = [z3, z3, z3, z3]
    if has_vres:
        z_specs.append(seq_spec)
        args.append(v_first)
    y_shape = jax.ShapeDtypeStruct((b, t, D_BRANCH), BF16)
    if has_vres:
        out_shape, out_specs = y_shape, seq_spec
    else:
        out_shape = (y_shape, jax.ShapeDtypeStruct((b, t, D_BRANCH), F32))
        out_specs = (seq_spec, seq_spec)
    return pl.pallas_call(
        functools.partial(_rwkv_kernel, has_vres),
        grid=(b, t // L),
        in_specs=z_specs + [full(a) for a in consts],
        out_specs=out_specs,
        out_shape=out_shape,
        scratch_shapes=[pltpu.VMEM((L + 8, R_COLS), F32),
                        pltpu.VMEM((R_HEADS // 2, 2 * R_HEAD_DIM, 2 * R_HEAD_DIM), F32)],
        compiler_params=_params(("parallel", "arbitrary")),
        name="rwkv7",
    )(*args, *consts)


def _pad_cols(a, n):
    return jnp.pad(a, ((0, 0), (0, n - a.shape[1])))


def _pad_rows(a, n, before=0):
    return jnp.pad(a, ((before, n - a.shape[0] - before), (0, 0)))


def _z_tile_table():
    off = dict(zip(("mq", "mk", "mv", "mo", "mi", "mf", "rz", "gq", "gk", "gv", "gz", "go", "gates"),
                   (0,) + IN_SPLITS))
    r0 = off["rz"]
    segs = [(off["mv"], 1024, 1024), (off["mo"], 1024, 1024),
            (r0, 1024, 1024), (r0 + 1024, 1024, 1024), (r0 + 2048, 1024, 1024),
            (off["gv"], 1024, 1024), (off["go"], 1024, 1024),
            (off["mq"], 1024, 1024), (off["gq"], 1024, 1024),
            (r0 + 3072, 256, 256), (off["mi"], 2 * M_HEADS, LANE), (off["gz"], G_RANK, LANE),
            (off["gates"], N_BRANCH * D_MODEL, N_BRANCH * D_MODEL)]
    start, nvalid = [], []
    for src, valid, width in segs:
        for t in range(width // LANE):
            assert (src + LANE * t) % 8 == 0
            start.append(src + LANE * t)
            nvalid.append(max(0, min(LANE, valid - LANE * t)))
    assert len(start) * LANE == Z_COLS
    return start, nvalid


def _prep_w_in_kernel(start_ref, nvalid_ref, src_ref, o_ref):
    del start_ref
    r = _iota((LANE, LANE), 0)
    c = _iota((LANE, LANE), 1)
    select = ((r == c) & (c < nvalid_ref[pl.program_id(0)])).astype(BF16)
    o_ref[...] = _dotb_tn(src_ref[...], select).astype(o_ref.dtype)


def _prep_w_in(w_all_t, layer):
    n_layers, n_src, d = w_all_t.shape
    start, nvalid = (jnp.asarray(a, jnp.int32) for a in _z_tile_table())
    grid_spec = pltpu.PrefetchScalarGridSpec(
        num_scalar_prefetch=2,
        grid=(Z_COLS // LANE,),
        in_specs=[pl.BlockSpec((pl.Element(LANE), pl.Element(d)), lambda j, s, n: (s[j] * 8, 0))],
        out_specs=pl.BlockSpec((d, LANE), lambda j, s, n: (0, j)),
    )
    return pl.pallas_call(
        _prep_w_in_kernel,
        grid_spec=grid_spec,
        out_shape=jax.ShapeDtypeStruct((d, Z_COLS), BF16),
        compiler_params=_params(("parallel",)),
        name="prep_w_in",
    )((start + layer * n_src) // 8, nvalid, w_all_t.reshape(n_layers * n_src, d))


def _row(a):
    return a.reshape(1, -1).astype(F32)


def _pick_tm(m, cap):
    tm = min(m, cap)
    while m % tm:
        tm //= 2
    return tm


def kernel(x, ffn1_norm, ffn1_w_in, ffn1_w_out, mix_norm, w_in, m_conv, m_i_bias, m_f_bias, m_norm, r_mu, r_w0, r_w2, r_a0, r_a2, r_g2, r_k_k, r_k_a, r_r_k, r_ln_w, r_ln_b, r_v0, r_v1, r_v2, g_gk_up, g_gk_bias, g_norm, w_branch, w_out, ffn2_norm, ffn2_w_in, ffn2_w_out, final_norm):
    bsz, seq, d = x.shape
    m = bsz * seq
    depth = w_in.shape[0]
    xs = x.reshape(m, d)
    tm_big = _pick_tm(m, 2048)
    tm_mid = _pick_tm(m, 1024)
    tm_small = _pick_tm(m, 512)

    def ffn(xs, norm_w, w1_all, w2_all, layer):
        h = _rmsnorm(xs, norm_w, BF16)
        act = _ffn_in(h, w1_all, layer, tm_big)
        return _matmul_res(act, w2_all, layer, xs, 0.5, tm_small, 512, "ffn_out")

    ffn1_w_out_b = ffn1_w_out.astype(BF16)
    ffn2_w_out_b = ffn2_w_out.astype(BF16)
    w_branch_b = w_branch.astype(BF16)
    w_out_b = w_out.astype(BF16)
    w_in_t = jnp.swapaxes(w_in, 1, 2)

    v_first = None
    for l in range(depth):
        xs = ffn(xs, ffn1_norm[l], ffn1_w_in, ffn1_w_out_b, l)

        z2 = _norm_matmul(xs, mix_norm[l], _prep_w_in(w_in_t, l), tm_mid, 512, F32)
        z3 = z2.reshape(bsz, seq, Z_COLS)

        gate_bias = _pad_cols(jnp.concatenate([m_i_bias[l], m_f_bias[l]]).reshape(1, -1), LANE)
        y_m = _mlstm(z3, m_conv[l], gate_bias, _row(m_norm[l]))

        mu = r_mu[l]
        rp = {
            "mu": _row(mu[:3 * D_BRANCH]), "mul": _row(mu[3 * D_BRANCH:]),
            "w0": _row(r_w0[l]), "w2": _pad_rows(r_w2[l], LANE).astype(BF16),
            "a0": _row(r_a0[l]), "a2": _pad_rows(r_a2[l], LANE, before=R_W_RANK).astype(BF16),
            "g2": r_g2[l].astype(BF16),
            "kk": _row(r_k_k[l]), "ka": _row(r_k_a[l]), "rk": _row(r_r_k[l]),
            "lnw": _row(r_ln_w[l]), "lnb": _row(r_ln_b[l]),
        }
        if l == 0:
            y_r, v_first = _rwkv(z3, None, rp)
        else:
            rp["v0"] = _row(r_v0[l - 1])
            rp["v1"] = _pad_cols(r_v1[l - 1], LANE).astype(BF16)
            rp["v2"] = _pad_rows(r_v2[l - 1], LANE).astype(BF16)
            y_r = _rwkv(z3, v_first, rp)

        y_g = _gla(z3, _pad_rows(g_gk_up[l], LANE), _row(g_gk_bias[l]), _row(g_norm[l]))

        merged = _merge(y_m.reshape(m, D_BRANCH), y_r.reshape(m, D_BRANCH), y_g.reshape(m, D_BRANCH),
                        w_branch_b, l, z2, tm_mid)
        xs = _matmul_res(merged, w_out_b, l, xs, 1.0, tm_mid, 512, "mix_out")

        xs = ffn(xs, ffn2_norm[l], ffn2_w_in, ffn2_w_out_b, l)

    out = _rmsnorm(xs, final_norm, F32)
    return out.reshape(bsz, seq, d)
```
